```python
import math
import numpy as np
import jax
import jax.numpy as jnp
from jax import lax

D_MODEL = 4096
BATCH = 2
SEQ = 8192
DEPTH = 4

GRID_W = 64
CTX_LEN = 256
CHUNK = 64
CONV_W = 5
H_A = 16
DK_A = 128
DV_A = 128
H_B = 8
DK_B = 256
DV_B = 256
W_A = H_A * DV_A
W_B = H_B * DV_B
W_CONV = 2 * H_A * DK_A + H_A * DV_A
R_MOD = 512
D_FF = 4096
N_EXPERTS = 8
TOP_K = 2
D_FF_EXPERT = 1024
ROPE_BASE = 10000.0
EPS = 1e-6
GN_EPS = 1e-5
IN_SIZES = (W_CONV, W_A, 4 * H_A, H_B * DK_B, H_B * DK_B, W_B, W_B, D_MODEL, D_MODEL)
N_IN = sum(IN_SIZES)

kernel_name = "hybrid_deltanet_retention_moe_dit"


def rmsnorm(t, w):
    t32 = t.astype(jnp.float32)
    y = t32 * lax.rsqrt(jnp.mean(t32 * t32, axis=-1, keepdims=True) + EPS)
    return (y * w.astype(jnp.float32)).astype(t.dtype)


def l2norm(t):
    return t * lax.rsqrt(jnp.sum(t * t, axis=-1, keepdims=True) + EPS)


def head_group_norm(o, w, b):
    mu = jnp.mean(o, axis=-1, keepdims=True)
    d = o - mu
    y = d * lax.rsqrt(jnp.mean(d * d, axis=-1, keepdims=True) + GN_EPS)
    return y.reshape(o.shape[:2] + (-1,)) * w.astype(jnp.float32) + b.astype(jnp.float32)


def rev(t):
    return jnp.flip(t, axis=1)


def modulation(cond, w_down, w_up, bias):
    m = (jax.nn.silu(cond) @ w_down) @ w_up + bias
    return [p[:, None, :] for p in jnp.split(m, 6, axis=-1)]


def modulate(t, gain, shift, scale):
    return rmsnorm(t, gain) * (1.0 + scale) + shift


def short_conv(t, w):
    pad = CONV_W // 2
    return lax.conv_general_dilated(
        t, w[:, None, :].astype(t.dtype), window_strides=(1,), padding=[(pad, pad)],
        dimension_numbers=("NWC", "WIO", "NWC"), feature_group_count=t.shape[-1])


def axial_angles(rows):
    n_freq = DK_B // 4
    freqs = ROPE_BASE ** (-jnp.arange(n_freq, dtype=jnp.float32) / n_freq)
    r = jnp.repeat(jnp.arange(rows, dtype=jnp.float32), GRID_W)
    col = jnp.tile(jnp.arange(GRID_W, dtype=jnp.float32), rows)
    return jnp.concatenate([r[:, None] * freqs, col[:, None] * freqs], axis=-1)


def rotate(t, ang):
    cos = jnp.cos(ang)[None, :, None, :]
    sin = jnp.sin(ang)[None, :, None, :]
    t1, t2 = jnp.split(t, 2, axis=-1)
    return jnp.concatenate([t1 * cos - t2 * sin, t2 * cos + t1 * sin], axis=-1)


def retention_log_decays():
    h = jnp.arange(H_B, dtype=jnp.float32)
    fwd = jnp.log1p(-jnp.exp2(-5.0 - h))
    bwd = jnp.log1p(-jnp.exp2(-5.5 - h))
    return fwd, bwd


def to_chunks(t):
    b, l = t.shape[:2]
    t = t.reshape((b, l // CHUNK, CHUNK) + t.shape[2:])
    return jnp.moveaxis(t, (1, 3), (0, 2))


def from_chunks(t):
    t = jnp.moveaxis(t, (0, 2), (1, 3))
    return t.reshape((t.shape[0], -1) + t.shape[3:])


def gated_delta_chunked(q, k, v, g, beta, s0):
    dk = q.shape[-1]
    dv = v.shape[-1]
    qc = to_chunks(q) * dk ** -0.5
    kc = to_chunks(k)
    vc = to_chunks(v)
    bc = to_chunks(beta)[..., None]
    gc = jnp.cumsum(to_chunks(g), axis=-1)
    incl = jnp.tril(jnp.ones((CHUNK, CHUNK), dtype=bool))
    strict = jnp.tril(jnp.ones((CHUNK, CHUNK), dtype=bool), -1)
    decay = jnp.exp(jnp.where(incl, gc[..., :, None] - gc[..., None, :], -jnp.inf))
    kk = jnp.einsum("nbhid,nbhjd->nbhij", kc * bc, kc) * decay
    lower = jnp.where(strict, kk, 0.0) + jnp.eye(CHUNK, dtype=kk.dtype)
    rhs = jnp.concatenate([vc * bc, kc * bc * jnp.exp(gc)[..., None]], axis=-1)
    sol = lax.linalg.triangular_solve(lower, rhs, left_side=True, lower=True, unit_diagonal=True)
    u, w = sol[..., :dv], sol[..., dv:]
    attn = jnp.einsum("nbhid,nbhjd->nbhij", qc, kc) * decay
    q_dec = qc * jnp.exp(gc)[..., None]
    k_dec = kc * jnp.exp(gc[..., -1:] - gc)[..., None]
    g_end = jnp.exp(gc[..., -1])[..., None, None]

    def step(s, inp):
        u_n, w_n, a_n, qd_n, kd_n, ge_n = inp
        v_new = u_n - jnp.einsum("bhcd,bhde->bhce", w_n, s)
        o_n = jnp.einsum("bhcd,bhde->bhce", qd_n, s) + jnp.einsum("bhij,bhje->bhie", a_n, v_new)
        s = s * ge_n + jnp.einsum("bhcd,bhce->bhde", kd_n, v_new)
        return s, o_n

    s_end, o = lax.scan(step, s0, (u, w, attn, q_dec, k_dec, g_end))
    return from_chunks(o), s_end


def retention_chunked(q, k, v, log_gamma, s0):
    qc, kc, vc = to_chunks(q), to_chunks(k), to_chunks(v)
    pos = jnp.arange(CHUNK, dtype=jnp.float32)
    lg = log_gamma[:, None]
    incl = jnp.tril(jnp.ones((CHUNK, CHUNK), dtype=bool))
    decay = jnp.exp(jnp.where(incl, (pos[:, None] - pos[None, :]) * lg[..., None], -jnp.inf))
    o_intra = jnp.einsum("nbhij,nbhje->nbhie", jnp.einsum("nbhid,nbhjd->nbhij", qc, kc) * decay, vc)
    q_dec = qc * jnp.exp((pos + 1.0) * lg)[..., None]
    k_dec = kc * jnp.exp((CHUNK - 1.0 - pos) * lg)[..., None]
    g_chunk = jnp.exp(CHUNK * log_gamma)[:, None, None]

    def step(s, inp):
        qd_n, kd_n, v_n, oi_n = inp
        o_n = jnp.einsum("bhcd,bhde->bhce", qd_n, s) + oi_n
        s = s * g_chunk + jnp.einsum("bhcd,bhce->bhde", kd_n, v_n)
        return s, o_n

    s_end, o = lax.scan(step, s0, (q_dec, k_dec, vc, o_intra))
    return from_chunks(o), s_end


def hybrid_mixer(h, ang, init, w_in, conv_w, a_log, dt_bias, dn_norm_w, gn_w, gn_b, w_br_a, w_br_b, w_out):
    b, l, _ = h.shape
    f32 = jnp.float32
    split_at = np.cumsum(IN_SIZES)[:-1].tolist()
    qkv_a, z_a, ab_a, q_b, k_b, v_b, g_b, gate_a, gate_b = jnp.split(h @ w_in, split_at, axis=-1)
    qkv_a = jax.nn.silu(short_conv(qkv_a, conv_w)).astype(f32)
    q_a, k_a, v_a = jnp.split(qkv_a, [H_A * DK_A, 2 * H_A * DK_A], axis=-1)
    q_a = l2norm(q_a.reshape(b, l, H_A, DK_A))
    k_a = l2norm(k_a.reshape(b, l, H_A, DK_A))
    v_a = v_a.reshape(b, l, H_A, DV_A)
    ab_a = ab_a.astype(f32).reshape(b, l, 4, H_A)
    beta = jax.nn.sigmoid(ab_a[:, :, :2])
    g = -jnp.exp(a_log.astype(f32)) * jax.nn.softplus(ab_a[:, :, 2:] + dt_bias.astype(f32))
    o_f, s_af = gated_delta_chunked(q_a, k_a, v_a, g[:, :, 0], beta[:, :, 0], init[0])
    o_r, s_ar = gated_delta_chunked(rev(q_a), rev(k_a), rev(v_a), rev(g[:, :, 1]), rev(beta[:, :, 1]), init[1])
    o_a = rmsnorm(o_f + rev(o_r), dn_norm_w) * jax.nn.silu(z_a.astype(f32).reshape(b, l, H_A, DV_A))
    o_a = o_a.reshape(b, l, W_A).astype(h.dtype)
    q_b = q_b.astype(f32).reshape(b, l, H_B, DK_B)
    k_b = k_b.astype(f32).reshape(b, l, H_B, DK_B) * DK_B ** -0.5
    v_b = v_b.astype(f32).reshape(b, l, H_B, DV_B)
    if ang is not None:
        q_b = rotate(q_b, ang)
        k_b = rotate(k_b, ang)
    lg_f, lg_r = retention_log_decays()
    r_f, s_bf = retention_chunked(q_b, k_b, v_b, lg_f, init[2])
    r_r, s_br = retention_chunked(rev(q_b), rev(k_b), rev(v_b), lg_r, init[3])
    o_b = head_group_norm(r_f + rev(r_r), gn_w, gn_b) * jax.nn.silu(g_b.astype(f32))
    o_b = o_b.astype(h.dtype)
    merged = jax.nn.sigmoid(gate_a) * (o_a @ w_br_a) + jax.nn.sigmoid(gate_b) * (o_b @ w_br_b)
    return merged @ w_out, (s_af, s_ar, s_bf, s_br)


def swiglu(t, w_gate, w_up, w_down):
    return (jax.nn.silu(t @ w_gate) * (t @ w_up)) @ w_down


def moe_swiglu(t, w_router, w_gate, w_up, w_down):
    logits = (t @ w_router).astype(jnp.float32)
    top_v, top_i = lax.top_k(logits, TOP_K)
    probs = jax.nn.softmax(top_v, axis=-1)
    combine = jnp.sum(jax.nn.one_hot(top_i, N_EXPERTS, dtype=probs.dtype) * probs[..., None], axis=-2)
    a = jnp.einsum("bld,edf->blef", t, w_gate)
    u = jnp.einsum("bld,edf->blef", t, w_up)
    act = jax.nn.silu(a) * u * combine[..., None].astype(t.dtype)
    return jnp.einsum("blef,efd->bld", act, w_down)


def setup_inputs(seed: int = 0) -> dict:
    key = jax.random.key(seed)
    ks = jax.random.split(key, 28)
    f32 = jnp.float32
    n_dense = (DEPTH + 1) // 2
    n_moe = DEPTH // 2

    def nrm(k, shape, fan_in, scale=1.0):
        return jax.random.normal(k, shape, f32) * (scale * fan_in ** -0.5)

    dt = jnp.exp(jax.random.uniform(ks[11], (DEPTH, 2, H_A), f32, math.log(1e-3), math.log(1e-1)))
    return {
        "x": jax.random.normal(ks[0], (BATCH, SEQ, D_MODEL), f32),
        "c": jax.random.normal(ks[1], (BATCH, D_MODEL), f32),
        "ctx": jax.random.normal(ks[2], (BATCH, CTX_LEN, D_MODEL), f32),
        "c_ctx": jax.random.normal(ks[3], (D_MODEL,), f32),
        "mod_down": nrm(ks[4], (DEPTH, D_MODEL, R_MOD), D_MODEL),
        "mod_up": nrm(ks[5], (DEPTH, R_MOD, 6 * D_MODEL), R_MOD, 0.5),
        "mod_bias": 0.02 * jax.random.normal(ks[6], (DEPTH, 6 * D_MODEL), f32),
        "norm_gains": 1.0 + 0.02 * jax.random.normal(ks[7], (DEPTH, 4, D_MODEL), f32),
        "w_in": nrm(ks[8], (DEPTH, D_MODEL, N_IN), D_MODEL),
        "conv_w": nrm(ks[9], (DEPTH, CONV_W, W_CONV), CONV_W),
        "a_log": jnp.log(jax.random.uniform(ks[10], (DEPTH, 2, H_A), f32, 1.0, 16.0)),
        "dt_bias": dt + jnp.log(-jnp.expm1(-dt)),
        "dn_norm_w": 1.0 + 0.02 * jax.random.normal(ks[12], (DEPTH, DV_A), f32),
        "gn_w": 1.0 + 0.02 * jax.random.normal(ks[13], (DEPTH, W_B), f32),
        "gn_b": 0.02 * jax.random.normal(ks[14], (DEPTH, W_B), f32),
        "w_br_a": nrm(ks[15], (DEPTH, W_A, D_MODEL), W_A),
        "w_br_b": nrm(ks[16], (DEPTH, W_B, D_MODEL), W_B),
        "w_out": nrm(ks[17], (DEPTH, D_MODEL, D_MODEL), D_MODEL),
        "ffn_gate": nrm(ks[18], (n_dense, D_MODEL, D_FF), D_MODEL),
        "ffn_up": nrm(ks[19], (n_dense, D_MODEL, D_FF), D_MODEL),
        "ffn_down": nrm(ks[20], (n_dense, D_FF, D_MODEL), D_FF),
        "router": nrm(ks[21], (n_moe, D_MODEL, N_EXPERTS), D_MODEL),
        "moe_gate": nrm(ks[22], (n_moe, N_EXPERTS, D_MODEL, D_FF_EXPERT), D_MODEL),
        "moe_up": nrm(ks[23], (n_moe, N_EXPERTS, D_MODEL, D_FF_EXPERT), D_MODEL),
        "moe_down": nrm(ks[24], (n_moe, N_EXPERTS, D_FF_EXPERT, D_MODEL), D_FF_EXPERT),
    }


def reference(x, c, ctx, c_ctx, mod_down, mod_up, mod_bias, norm_gains, w_in, conv_w, a_log, dt_bias,
              dn_norm_w, gn_w, gn_b, w_br_a, w_br_b, w_out, ffn_gate, ffn_up, ffn_down,
              router, moe_gate, moe_up, moe_down):
    rows = x.shape[1] // GRID_W
    ang = axial_angles(rows)
    b = x.shape[0]
    zero_a = jnp.zeros((b, H_A, DK_A, DV_A), jnp.float32)
    zero_b = jnp.zeros((b, H_B, DK_B, DV_B), jnp.float32)
    zero_states = (zero_a, zero_a, zero_b, zero_b)
    cx = ctx
    for i in range(DEPTH):
        mx = modulation(c, mod_down[i], mod_up[i], mod_bias[i])
        mc = modulation(c_ctx[None, :], mod_down[i], mod_up[i], mod_bias[i])
        gains = norm_gains[i]
        mix_params = (w_in[i], conv_w[i], a_log[i], dt_bias[i], dn_norm_w[i], gn_w[i], gn_b[i],
                      w_br_a[i], w_br_b[i], w_out[i])
        j = i // 2

        def channel_mixer(t):
            if i % 2 == 0:
                return swiglu(t, ffn_gate[j], ffn_up[j], ffn_down[j])
            return moe_swiglu(t, router[j], moe_gate[j], moe_up[j], moe_down[j])

        yc, ctx_states = hybrid_mixer(modulate(cx, gains[0], mc[0], mc[1]), None, zero_states, *mix_params)
        yx, _ = hybrid_mixer(modulate(x, gains[0], mx[0], mx[1]), ang, ctx_states, *mix_params)
        x = x + mx[2] * rmsnorm(yx, gains[1])
        x = x + mx[5] * rmsnorm(channel_mixer(modulate(x, gains[2], mx[3], mx[4])), gains[3])
        if i < DEPTH - 1:
            cx = cx + mc[2] * rmsnorm(yc, gains[1])
            cx = cx + mc[5] * rmsnorm(channel_mixer(modulate(cx, gains[2], mc[3], mc[4])), gains[3])
    return x
```

```python
import functools

import jax
import jax.numpy as jnp
from jax import lax
from jax.experimental import pallas as pl
from jax.experimental.pallas import tpu as pltpu

GRID_W = 64
CONV_W = 5
H_A = 16
DK_A = 128
DV_A = 128
H_B = 8
DK_B = 256
DV_B = 256
TOP_K = 2
ROPE_BASE = 10000.0
EPS = 1e-6
GN_EPS = 1e-5

LANES = 128
ROW_BLOCK = 256
CHUNK_A = 64
AUX_W = LANES
VMEM_LIMIT = 56 * 1024 * 1024

F32 = jnp.float32
BF16 = jnp.bfloat16


def _cparams(sem):
    return pltpu.CompilerParams(dimension_semantics=sem, vmem_limit_bytes=VMEM_LIMIT)


def _silu(t):
    return t * jax.nn.sigmoid(t)


def _dot(a, b):
    return jnp.dot(a, b, preferred_element_type=F32)


def _dot_nt(a, b):
    return lax.dot_general(a, b, (((1,), (1,)), ((), ())), preferred_element_type=F32)


def _dot_tn(a, b):
    return lax.dot_general(a, b, (((0,), (0,)), ((), ())), preferred_element_type=F32)


def _split(a):
    hi = a.astype(BF16)
    lo = (a - hi.astype(F32)).astype(BF16)
    return hi, lo


def _dot3(a, b):
    ah, al = _split(a)
    bh, bl = _split(b)
    return _dot(ah, bh) + (_dot(ah, bl) + _dot(al, bh))


def _mod_kernel(cond_ref, wd_ref, wu_ref, b_ref, o_ref, t_ref):
    @pl.when(pl.program_id(1) == 0)
    def _():
        s = _silu(cond_ref[...])
        t_ref[...] = _dot(s.astype(BF16), wd_ref[0].astype(BF16))

    o_ref[0] = _dot(t_ref[...].astype(BF16), wu_ref[0].astype(BF16)) + b_ref[0]


def _modulation(cond, w_down, w_up, bias):
    depth, d, r = w_down.shape
    n = w_up.shape[2]
    tn = _pick(n, (2048, 1024, 512, 256, 128))
    return pl.pallas_call(
        _mod_kernel,
        out_shape=jax.ShapeDtypeStruct((depth, 8, n), F32),
        grid=(depth, n // tn),
        in_specs=[
            pl.BlockSpec((8, d), lambda l, j: (0, 0)),
            pl.BlockSpec((1, d, r), lambda l, j: (l, 0, 0)),
            pl.BlockSpec((1, r, tn), lambda l, j: (l, 0, j)),
            pl.BlockSpec((1, 1, tn), lambda l, j: (l, 0, j)),
        ],
        out_specs=pl.BlockSpec((1, 8, tn), lambda l, j: (l, 0, j)),
        scratch_shapes=[pltpu.VMEM((8, r), F32)],
        compiler_params=_cparams(("arbitrary", "arbitrary")),
        name="modulation",
    )(cond, w_down, w_up, bias.reshape(depth, 1, n))


def _rms(t, gain):
    return t * lax.rsqrt(jnp.mean(t * t, axis=-1, keepdims=True) + EPS) * gain


def _row_kernel(*refs, has_y, has_h, has_aux):
    it = iter(refs)
    x_ref = next(it)
    y_ref = next(it) if has_y else None
    mp_ref = next(it)
    gn_ref = next(it)
    waux_ref = next(it) if has_aux else None
    xo_ref = next(it) if has_y else None
    h_ref = next(it) if has_h else None
    aux_ref = next(it) if has_aux else None

    x = x_ref[...]
    if has_y:
        x = x + mp_ref[0, 0:1, :] * _rms(y_ref[...].astype(F32), gn_ref[0:1, :])
        xo_ref[...] = x
    if has_h:
        h = _rms(x, gn_ref[1:2, :]) * (1.0 + mp_ref[0, 2:3, :]) + mp_ref[0, 1:2, :]
        hb = h.astype(BF16)
        h_ref[...] = hb
        if has_aux:
            aux_ref[...] = _dot(hb, waux_ref[...])


def _row_stage(x, y, mp, gn, waux, nb, ncb, *, has_h):
    m, d = x.shape
    has_y = y is not None
    has_aux = waux is not None
    n_ctx_rows = mp.shape[0] - 1

    def sel(i):
        return jnp.where(i % nb < ncb, n_ctx_rows, i // nb)

    row = pl.BlockSpec((ROW_BLOCK, d), lambda i: (i, 0))
    in_specs = [row]
    args = [x]
    if has_y:
        in_specs.append(row)
        args.append(y)
    in_specs += [pl.BlockSpec((1, 3, d), lambda i: (sel(i), 0, 0)), pl.BlockSpec((2, d), lambda i: (0, 0))]
    args += [mp, gn]
    if has_aux:
        in_specs.append(pl.BlockSpec((d, AUX_W), lambda i: (0, 0)))
        args.append(waux)
    out_shape, out_specs = [], []
    if has_y:
        out_shape.append(jax.ShapeDtypeStruct((m, d), F32))
        out_specs.append(row)
    if has_h:
        out_shape.append(jax.ShapeDtypeStruct((m, d), BF16))
        out_specs.append(row)
    if has_aux:
        out_shape.append(jax.ShapeDtypeStruct((m, AUX_W), F32))
        out_specs.append(pl.BlockSpec((ROW_BLOCK, AUX_W), lambda i: (i, 0)))
    outs = pl.pallas_call(
        functools.partial(_row_kernel, has_y=has_y, has_h=has_h, has_aux=has_aux),
        out_shape=out_shape,
        grid=(m // ROW_BLOCK,),
        in_specs=in_specs,
        out_specs=out_specs,
        compiler_params=_cparams(("parallel",)),
        name="row_stage",
    )(*args)
    return list(outs)


def _mm_kernel(a_ref, w_ref, o_ref):
    o_ref[...] = _dot(a_ref[...], w_ref[...]).astype(o_ref.dtype)


def _pick(n, pref):
    for t in pref:
        if n % t == 0:
            return t
    return n


def _matmul(a, w, *, tm, tn, out_dtype=BF16, name="matmul"):
    m, k = a.shape
    n = w.shape[1]
    return pl.pallas_call(
        _mm_kernel,
        out_shape=jax.ShapeDtypeStruct((m, n), out_dtype),
        grid=(m // tm, n // tn),
        in_specs=[pl.BlockSpec((tm, k), lambda i, j: (i, 0)), pl.BlockSpec((k, tn), lambda i, j: (0, j))],
        out_specs=pl.BlockSpec((tm, tn), lambda i, j: (i, j)),
        compiler_params=_cparams(("parallel", "parallel")),
        name=name,
    )(a, w)


def _merge_kernel(oa_ref, wa_ref, ob_ref, wb_ref, ga_ref, gb_ref, o_ref):
    ya = _dot(oa_ref[...], wa_ref[...])
    yb = _dot(ob_ref[...], wb_ref[...])
    ga = jax.nn.sigmoid(ga_ref[...].astype(F32))
    gb = jax.nn.sigmoid(gb_ref[...].astype(F32))
    o_ref[...] = (ga * ya + gb * yb).astype(o_ref.dtype)


def _branch_merge(o_a, w_a, o_b, w_b, p, col_ga, col_gb, *, tm, tn):
    m, ka = o_a.shape
    kb = o_b.shape[1]
    n = w_a.shape[1]
    ja, jb = col_ga // tn, col_gb // tn
    return pl.pallas_call(
        _merge_kernel,
        out_shape=jax.ShapeDtypeStruct((m, n), BF16),
        grid=(m // tm, n // tn),
        in_specs=[
            pl.BlockSpec((tm, ka), lambda i, j: (i, 0)),
            pl.BlockSpec((ka, tn), lambda i, j: (0, j)),
            pl.BlockSpec((tm, kb), lambda i, j: (i, 0)),
            pl.BlockSpec((kb, tn), lambda i, j: (0, j)),
            pl.BlockSpec((tm, tn), lambda i, j: (i, ja + j)),
            pl.BlockSpec((tm, tn), lambda i, j: (i, jb + j)),
        ],
        out_specs=pl.BlockSpec((tm, tn), lambda i, j: (i, j)),
        compiler_params=_cparams(("parallel", "parallel")),
        name="branch_merge",
    )(o_a, w_a, o_b, w_b, p, p)


def _ffn_up_kernel(*refs, tn, f_expert, has_combine):
    if has_combine:
        h_ref, wg_ref, wu_ref, cmb_ref, o_ref = refs
    else:
        h_ref, wg_ref, wu_ref, o_ref = refs
    h = h_ref[...]
    act = _silu(_dot(h, wg_ref[...])) * _dot(h, wu_ref[...])
    if has_combine:
        e = (pl.program_id(1) * tn) // f_expert
        cmb = cmb_ref[...]
        lane = lax.broadcasted_iota(jnp.int32, cmb.shape, 1)
        act = act * jnp.sum(jnp.where(lane == e, cmb, 0.0), axis=-1, keepdims=True)
    o_ref[...] = act.astype(o_ref.dtype)


def _ffn_up(h, w_gate, w_up, combine, f_expert, *, tm, tn):
    m, k = h.shape
    n = w_gate.shape[1]
    has_combine = combine is not None
    in_specs = [
        pl.BlockSpec((tm, k), lambda i, j: (i, 0)),
        pl.BlockSpec((k, tn), lambda i, j: (0, j)),
        pl.BlockSpec((k, tn), lambda i, j: (0, j)),
    ]
    args = [h, w_gate, w_up]
    if has_combine:
        in_specs.append(pl.BlockSpec((tm, AUX_W), lambda i, j: (i, 0)))
        args.append(combine)
    return pl.pallas_call(
        functools.partial(_ffn_up_kernel, tn=tn, f_expert=f_expert, has_combine=has_combine),
        out_shape=jax.ShapeDtypeStruct((m, n), BF16),
        grid=(m // tm, n // tn),
        in_specs=in_specs,
        out_specs=pl.BlockSpec((tm, tn), lambda i, j: (i, j)),
        compiler_params=_cparams(("parallel", "parallel")),
        name="ffn_up",
    )(*args)


def _router_kernel(lg_ref, o_ref, *, n_experts):
    lg = lg_ref[...]
    lane = lax.broadcasted_iota(jnp.int32, lg.shape, 1)
    neg = jnp.float32(-jnp.inf)
    valid = lane < n_experts
    v = jnp.where(valid, lg, neg)
    m1 = jnp.max(v, axis=-1, keepdims=True)
    i1 = jnp.min(jnp.where(v == m1, lane, AUX_W), axis=-1, keepdims=True)
    v2 = jnp.where(lane == i1, neg, v)
    m2 = jnp.max(v2, axis=-1, keepdims=True)
    i2 = jnp.min(jnp.where(v2 == m2, lane, AUX_W), axis=-1, keepdims=True)
    e2 = jnp.exp(m2 - m1)
    p1 = 1.0 / (1.0 + e2)
    p2 = e2 / (1.0 + e2)
    o_ref[...] = jnp.where(lane == i1, p1, 0.0) + jnp.where(lane == i2, p2, 0.0)


def _router(logits, n_experts):
    m = logits.shape[0]
    blk = pl.BlockSpec((ROW_BLOCK, AUX_W), lambda i: (i, 0))
    return pl.pallas_call(
        functools.partial(_router_kernel, n_experts=n_experts),
        out_shape=jax.ShapeDtypeStruct((m, AUX_W), F32),
        grid=(m // ROW_BLOCK,),
        in_specs=[blk],
        out_specs=blk,
        compiler_params=_cparams(("parallel",)),
        name="router_top2",
    )(logits)


def _gates_kernel(ab_ref, par_ref, o_ref):
    x = ab_ref[...]
    rows, width = x.shape
    lane = lax.broadcasted_iota(jnp.int32, x.shape, 1)
    pos = lax.broadcasted_iota(jnp.int32, x.shape, 0) % CHUNK_A
    beta = jax.nn.sigmoid(x)
    z = x + par_ref[1:2, :]
    softplus = jnp.maximum(z, 0.0) + jnp.log1p(jnp.exp(-jnp.abs(z)))
    g = -jnp.exp(par_ref[0:1, :]) * softplus
    pre = g
    suf = g
    s = 1
    while s < CHUNK_A:
        pre = pre + jnp.where(pos >= s, pltpu.roll(pre, s, 0), 0.0)
        suf = suf + jnp.where(pos < CHUNK_A - s, pltpu.roll(suf, rows - s, 0), 0.0)
        s *= 2
    o_ref[...] = jnp.where(lane < 2 * H_A, beta, jnp.where(lane < 3 * H_A, pre, suf))


def _gates(ab, a_log, dt_bias):
    m = ab.shape[0]
    par = jnp.zeros((2, AUX_W), F32)
    par = par.at[0, 2 * H_A:4 * H_A].set(a_log.reshape(-1).astype(F32))
    par = par.at[1, 2 * H_A:4 * H_A].set(dt_bias.reshape(-1).astype(F32))
    blk = pl.BlockSpec((ROW_BLOCK, AUX_W), lambda i: (i, 0))
    return pl.pallas_call(
        _gates_kernel,
        out_shape=jax.ShapeDtypeStruct((m, AUX_W), F32),
        grid=(m // ROW_BLOCK,),
        in_specs=[blk, pl.BlockSpec((2, AUX_W), lambda i: (0, 0))],
        out_specs=blk,
        compiler_params=_cparams(("parallel",)),
        name="delta_gates",
    )(ab, par)


_HALO = 16


def _conv_kernel(prev_ref, cur_ref, next_ref, w_ref, o_ref, *, cb, nb, ncb, n_qk_blocks, n_q_blocks):
    t = pl.program_id(1)
    c = pl.program_id(2)
    prev_ok = jnp.logical_and(t != 0, t != ncb)
    next_ok = jnp.logical_and(t != ncb - 1, t != nb - 1)
    prev = jnp.where(prev_ok, prev_ref[0].astype(F32), 0.0)
    nxt = jnp.where(next_ok, next_ref[0].astype(F32), 0.0)
    x = jnp.concatenate([prev, cur_ref[0].astype(F32), nxt], axis=0)
    rows = x.shape[0]
    w = w_ref[...]
    y = None
    for d in range(CONV_W):
        shift = (CONV_W // 2 - d) % rows
        xs = x if shift == 0 else pltpu.roll(x, shift, 0)
        term = xs[_HALO:_HALO + ROW_BLOCK] * w[d:d + 1, :]
        y = term if y is None else y + term
    y = _silu(y)
    is_qk = c < n_qk_blocks
    q_scale = jnp.where(c < n_q_blocks, jnp.float32(DK_A ** -0.5), jnp.float32(1.0))
    for hh in range(cb // DK_A):
        s = y[:, hh * DK_A:(hh + 1) * DK_A]
        inv = lax.rsqrt(jnp.sum(s * s, axis=-1, keepdims=True) + EPS) * q_scale
        o_ref[0, :, hh * DK_A:(hh + 1) * DK_A] = (s * jnp.where(is_qk, inv, 1.0)).astype(o_ref.dtype)


def _conv_prep(p3, conv_w, nb, ncb, cb=512):
    b, lt, _ = p3.shape
    wc = conv_w.shape[1]
    per = ROW_BLOCK // _HALO
    n_halo = lt // _HALO
    kern = functools.partial(_conv_kernel, cb=cb, nb=nb, ncb=ncb,
                             n_qk_blocks=2 * H_A * DK_A // cb, n_q_blocks=H_A * DK_A // cb)
    return pl.pallas_call(
        kern,
        out_shape=jax.ShapeDtypeStruct((b, lt, wc), BF16),
        grid=(b, nb, wc // cb),
        in_specs=[
            pl.BlockSpec((1, _HALO, cb), lambda i, t, c: (i, jnp.maximum(t * per - 1, 0), c)),
            pl.BlockSpec((1, ROW_BLOCK, cb), lambda i, t, c: (i, t, c)),
            pl.BlockSpec((1, _HALO, cb), lambda i, t, c: (i, jnp.minimum((t + 1) * per, n_halo - 1), c)),
            pl.BlockSpec((CONV_W, cb), lambda i, t, c: (0, c)),
        ],
        out_specs=pl.BlockSpec((1, ROW_BLOCK, cb), lambda i, t, c: (i, t, c)),
        compiler_params=_cparams(("parallel", "parallel", "parallel")),
        name="conv_prep",
    )(p3, p3, p3, conv_w)


def _delta_chunk(q, k, v, beta, gc, gc_row, s_ref, reverse):
    c = q.shape[0]
    row = lax.broadcasted_iota(jnp.int32, (c, c), 0)
    col = lax.broadcasted_iota(jnp.int32, (c, c), 1)
    if reverse:
        incl, strict, last = row <= col, row < col, 0
    else:
        incl, strict, last = row >= col, row > col, c - 1
    decay = jnp.exp(jnp.where(incl, gc - gc_row, -1e30))
    kf = k.astype(F32)
    kk = _dot_nt(k, k)
    qk = _dot_nt(q, k)
    n = jnp.where(strict, beta * kk * decay, 0.0)
    eye = (row == col).astype(F32)
    t_inv = eye - n
    pw = n
    span = 2
    while span < c:
        pw = _dot3(pw, pw)
        t_inv = t_inv + _dot3(t_inv, pw)
        span *= 2
    e = jnp.exp(gc)
    kb = kf * beta
    th, tl = _split(t_inv)
    vb = (v.astype(F32) * beta).astype(BF16)
    kbe = (kb * e).astype(BF16)
    u = _dot(th, vb) + _dot(tl, vb)
    w = _dot(th, kbe) + _dot(tl, kbe)
    g_last = gc[last:last + 1, :]
    s = s_ref[...]
    sb = s.astype(BF16)
    v_new = u - _dot(w.astype(BF16), sb)
    vnb = v_new.astype(BF16)
    q_dec = (q.astype(F32) * e).astype(BF16)
    o = _dot(q_dec, sb) + _dot((qk * decay).astype(BF16), vnb)
    k_dec = (kf * jnp.exp(g_last - gc)).astype(BF16)
    s_ref[...] = s * jnp.exp(g_last) + _dot_tn(k_dec, vnb)
    return o


def _delta_kernel(qf_ref, kf_ref, vf_ref, gcf_ref, grf_ref, qb_ref, kb_ref, vb_ref, gcb_ref, grb_ref,
                  of_ref, ob_ref, sf_ref, sb_ref):
    @pl.when(pl.program_id(2) == 0)
    def _():
        sf_ref[...] = jnp.zeros_like(sf_ref)
        sb_ref[...] = jnp.zeros_like(sb_ref)

    nchunks = ROW_BLOCK // CHUNK_A
    for ci in range(nchunks):
        for reverse in (False, True):
            cc = nchunks - 1 - ci if reverse else ci
            rows = slice(cc * CHUNK_A, (cc + 1) * CHUNK_A)
            q_ref, k_ref, v_ref, gc_ref, gr_ref, o_ref, s_ref = (
                (qb_ref, kb_ref, vb_ref, gcb_ref, grb_ref, ob_ref, sb_ref) if reverse else
                (qf_ref, kf_ref, vf_ref, gcf_ref, grf_ref, of_ref, sf_ref))
            bcol = 1 if reverse else 0
            gcol = 3 if reverse else 2
            gcols = gc_ref[0, 0, rows, :]
            o = _delta_chunk(
                q_ref[0, rows, :], k_ref[0, rows, :], v_ref[0, rows, :],
                gcols[:, bcol:bcol + 1], gcols[:, gcol:gcol + 1],
                gr_ref[0, 0, cc, gcol:gcol + 1, :], s_ref, reverse)
            o_ref[0, rows, :] = o


def _bwd_block(t, nb, ncb):
    return jnp.where(t < ncb, ncb - 1 - t, nb - 1 - (t - ncb))


def _delta_scan(qkv, g_col, g_row, nb, ncb):
    b, lt, _ = qkv.shape
    nch = ROW_BLOCK // CHUNK_A

    def tok(col0, rev):
        if rev:
            return pl.BlockSpec((1, ROW_BLOCK, DK_A), lambda i, h, t: (i, _bwd_block(t, nb, ncb), col0 + h))
        return pl.BlockSpec((1, ROW_BLOCK, DK_A), lambda i, h, t: (i, t, col0 + h))

    def gcol(rev):
        if rev:
            return pl.BlockSpec((1, 1, ROW_BLOCK, 4), lambda i, h, t: (i, h, _bwd_block(t, nb, ncb), 0))
        return pl.BlockSpec((1, 1, ROW_BLOCK, 4), lambda i, h, t: (i, h, t, 0))

    def grow(rev):
        if rev:
            return pl.BlockSpec((1, 1, nch, 4, CHUNK_A), lambda i, h, t: (i, h, _bwd_block(t, nb, ncb), 0, 0))
        return pl.BlockSpec((1, 1, nch, 4, CHUNK_A), lambda i, h, t: (i, h, t, 0, 0))

    def out(rev):
        if rev:
            return pl.BlockSpec((1, ROW_BLOCK, DV_A), lambda i, h, t: (i, _bwd_block(t, nb, ncb), h))
        return pl.BlockSpec((1, ROW_BLOCK, DV_A), lambda i, h, t: (i, t, h))

    in_specs, args = [], []
    for rev in (False, True):
        in_specs += [tok(0, rev), tok(H_A, rev), tok(2 * H_A, rev), gcol(rev), grow(rev)]
        args += [qkv, qkv, qkv, g_col, g_row]
    o_shape = jax.ShapeDtypeStruct((b, lt, H_A * DV_A), F32)
    return pl.pallas_call(
        _delta_kernel,
        out_shape=[o_shape, o_shape],
        grid=(b, H_A, nb),
        in_specs=in_specs,
        out_specs=[out(False), out(True)],
        scratch_shapes=[pltpu.VMEM((DK_A, DV_A), F32), pltpu.VMEM((DK_A, DV_A), F32)],
        compiler_params=_cparams(("parallel", "parallel", "arbitrary")),
        name="delta_scan",
    )(*args)


def _delta_out_kernel(of_ref, ob_ref, z_ref, w_ref, o_ref, *, cb):
    o = of_ref[...] + ob_ref[...]
    z = z_ref[...].astype(F32)
    w = w_ref[...]
    for hh in range(cb // DV_A):
        sl = slice(hh * DV_A, (hh + 1) * DV_A)
        s = o[:, sl]
        y = s * lax.rsqrt(jnp.mean(s * s, axis=-1, keepdims=True) + EPS) * w
        o_ref[:, sl] = (y * _silu(z[:, sl])).astype(o_ref.dtype)


def _delta_out(o_f, o_b, p, col_z, dn_norm_w, cb=512):
    m, wa = o_f.shape
    blk = pl.BlockSpec((ROW_BLOCK, cb), lambda i, c: (i, c))
    j0 = col_z // cb
    return pl.pallas_call(
        functools.partial(_delta_out_kernel, cb=cb),
        out_shape=jax.ShapeDtypeStruct((m, wa), BF16),
        grid=(m // ROW_BLOCK, wa // cb),
        in_specs=[blk, blk, pl.BlockSpec((ROW_BLOCK, cb), lambda i, c: (i, j0 + c)),
                  pl.BlockSpec((1, DV_A), lambda i, c: (0, 0))],
        out_specs=blk,
        compiler_params=_cparams(("parallel", "parallel")),
        name="delta_out",
    )(o_f, o_b, p, dn_norm_w.reshape(1, DV_A).astype(F32))


def _rotate(t, cos, sin):
    half = t.shape[-1] // 2
    t1, t2 = t[:, :half], t[:, half:]
    return jnp.concatenate([t1 * cos - t2 * sin, t2 * cos + t1 * sin], axis=-1)


def _ret_dir(q, k, v, lg, s_ref, reverse):
    c = q.shape[0]
    row = lax.broadcasted_iota(jnp.int32, (c, c), 0)
    col = lax.broadcasted_iota(jnp.int32, (c, c), 1)
    dist = (col - row) if reverse else (row - col)
    decay = jnp.exp(jnp.where(dist >= 0, dist.astype(F32) * lg, -1e30))
    pos = lax.broadcasted_iota(jnp.int32, (c, 1), 0)
    pos = (c - 1 - pos) if reverse else pos
    posf = pos.astype(F32)
    qb = q.astype(BF16)
    kb = k.astype(BF16)
    s = s_ref[...]
    o = _dot((_dot_nt(qb, kb) * decay).astype(BF16), v)
    o = o + _dot((q * jnp.exp((posf + 1.0) * lg)).astype(BF16), s.astype(BF16))
    k_dec = (k * jnp.exp((c - 1.0 - posf) * lg)).astype(BF16)
    s_ref[...] = s * jnp.exp(c * lg) + _dot_tn(k_dec, v)
    return o


def _ret_kernel(qf_ref, kf_ref, vf_ref, cf_ref, sf_ref, qb_ref, kb_ref, vb_ref, cb_ref, sb_ref, lg_ref,
                of_ref, ob_ref, stf_ref, stb_ref):
    @pl.when(pl.program_id(2) == 0)
    def _():
        stf_ref[...] = jnp.zeros_like(stf_ref)
        stb_ref[...] = jnp.zeros_like(stb_ref)

    k_scale = DK_B ** -0.5
    for reverse in (False, True):
        q_ref, k_ref, v_ref, c_ref, s_ref, o_ref, st_ref = (
            (qb_ref, kb_ref, vb_ref, cb_ref, sb_ref, ob_ref, stb_ref) if reverse else
            (qf_ref, kf_ref, vf_ref, cf_ref, sf_ref, of_ref, stf_ref))
        cos, sin = c_ref[...], s_ref[...]
        q = _rotate(q_ref[0].astype(F32), cos, sin)
        k = _rotate(k_ref[0].astype(F32) * k_scale, cos, sin)
        lg = lg_ref[0, 1 if reverse else 0, 0:1, 0:1]
        o_ref[0] = _ret_dir(q, k, v_ref[0], lg, st_ref, reverse)


def _ret_scan(p3, col_q, col_k, col_v, cos, sin, lg, nb, ncb):
    b, lt, _ = p3.shape
    jq, jk, jv = col_q // DK_B, col_k // DK_B, col_v // DV_B

    def tok(j0, rev):
        if rev:
            return pl.BlockSpec((1, ROW_BLOCK, DK_B), lambda i, h, t: (i, _bwd_block(t, nb, ncb), j0 + h))
        return pl.BlockSpec((1, ROW_BLOCK, DK_B), lambda i, h, t: (i, t, j0 + h))

    def ang(rev):
        if rev:
            return pl.BlockSpec((ROW_BLOCK, DK_B // 2), lambda i, h, t: (_bwd_block(t, nb, ncb), 0))
        return pl.BlockSpec((ROW_BLOCK, DK_B // 2), lambda i, h, t: (t, 0))

    in_specs, args = [], []
    for rev in (False, True):
        in_specs += [tok(jq, rev), tok(jk, rev), tok(jv, rev), ang(rev), ang(rev)]
        args += [p3, p3, p3, cos, sin]
    in_specs.append(pl.BlockSpec((1, 2, 8, LANES), lambda i, h, t: (h, 0, 0, 0)))
    args.append(lg)
    o_shape = jax.ShapeDtypeStruct((b, lt, H_B * DV_B), F32)
    return pl.pallas_call(
        _ret_kernel,
        out_shape=[o_shape, o_shape],
        grid=(b, H_B, nb),
        in_specs=in_specs,
        out_specs=[tok(0, False), tok(0, True)],
        scratch_shapes=[pltpu.VMEM((DK_B, DV_B), F32), pltpu.VMEM((DK_B, DV_B), F32)],
        compiler_params=_cparams(("parallel", "parallel", "arbitrary")),
        name="retention_scan",
    )(*args)


def _ret_out_kernel(of_ref, ob_ref, g_ref, w_ref, b_ref, o_ref, *, cb):
    o = of_ref[...] + ob_ref[...]
    g = g_ref[...].astype(F32)
    for hh in range(cb // DV_B):
        sl = slice(hh * DV_B, (hh + 1) * DV_B)
        s = o[:, sl]
        d = s - jnp.mean(s, axis=-1, keepdims=True)
        y = d * lax.rsqrt(jnp.mean(d * d, axis=-1, keepdims=True) + GN_EPS)
        o_ref[:, sl] = ((y * w_ref[:, sl] + b_ref[:, sl]) * _silu(g[:, sl])).astype(o_ref.dtype)


def _ret_out(o_f, o_b, p, col_g, gn_w, gn_b, cb=512):
    m, wb = o_f.shape
    blk = pl.BlockSpec((ROW_BLOCK, cb), lambda i, c: (i, c))
    par = pl.BlockSpec((1, cb), lambda i, c: (0, c))
    j0 = col_g // cb
    return pl.pallas_call(
        functools.partial(_ret_out_kernel, cb=cb),
        out_shape=jax.ShapeDtypeStruct((m, wb), BF16),
        grid=(m // ROW_BLOCK, wb // cb),
        in_specs=[blk, blk, pl.BlockSpec((ROW_BLOCK, cb), lambda i, c: (i, j0 + c)), par, par],
        out_specs=blk,
        compiler_params=_cparams(("parallel", "parallel")),
        name="retention_out",
    )(o_f, o_b, p, gn_w.reshape(1, wb).astype(F32), gn_b.reshape(1, wb).astype(F32))


def _rope_tables(ctx_len, seq):
    n_freq = DK_B // 4
    freqs = ROPE_BASE ** (-jnp.arange(n_freq, dtype=F32) / n_freq)
    rows = seq // GRID_W
    r = jnp.repeat(jnp.arange(rows, dtype=F32), GRID_W)
    col = jnp.tile(jnp.arange(GRID_W, dtype=F32), rows)
    ang = jnp.concatenate([r[:, None] * freqs, col[:, None] * freqs], axis=-1)
    ang = jnp.concatenate([jnp.zeros((ctx_len, DK_B // 2), F32), ang], axis=0)
    return jnp.cos(ang), jnp.sin(ang)


def _retention_log_decays():
    h = jnp.arange(H_B, dtype=F32)
    fwd = jnp.log1p(-jnp.exp2(-5.0 - h))
    bwd = jnp.log1p(-jnp.exp2(-5.5 - h))
    lg = jnp.stack([fwd, bwd], axis=1)
    return jnp.broadcast_to(lg[:, :, None, None], (H_B, 2, 8, LANES))


def kernel(x, c, ctx, c_ctx, mod_down, mod_up, mod_bias, norm_gains, w_in, conv_w, a_log, dt_bias, dn_norm_w,
           gn_w, gn_b, w_br_a, w_br_b, w_out, ffn_gate, ffn_up, ffn_down, router, moe_gate, moe_up, moe_down):
    b, seq, d = x.shape
    ctx_len = ctx.shape[1]
    depth = w_in.shape[0]
    lt = ctx_len + seq
    assert ctx_len % ROW_BLOCK == 0 and seq % ROW_BLOCK == 0 and seq % GRID_W == 0 and b + 1 <= 8
    nb, ncb = lt // ROW_BLOCK, ctx_len // ROW_BLOCK
    m = b * lt
    w_conv = 2 * H_A * DK_A + H_A * DV_A
    w_a, w_b = H_A * DV_A, H_B * DV_B
    n_ab = 4 * H_A
    col_z = w_conv
    col_qb = col_z + w_a
    col_kb = col_qb + H_B * DK_B
    col_vb = col_kb + H_B * DK_B
    col_gb = col_vb + w_b
    col_ga = col_gb + w_b
    col_gate_b = col_ga + d
    n_main = col_gate_b + d

    tm = _pick(m, (768, 512, 256))

    cond = jnp.zeros((8, d), F32).at[:b].set(c).at[b].set(c_ctx)
    mod = _modulation(cond, mod_down, mod_up, mod_bias)[:, :b + 1].reshape(depth, b + 1, 6, d)

    cos, sin = _rope_tables(ctx_len, seq)
    lg = _retention_log_decays()

    xs = jnp.concatenate([ctx, x], axis=1).reshape(m, d)

    def in_weights(i):
        w = w_in[i]
        w_main = jnp.concatenate([w[:, :col_qb], w[:, col_qb + n_ab:]], axis=1).astype(BF16)
        w_ab = jnp.zeros((d, AUX_W), BF16).at[:, :n_ab].set(w[:, col_qb:col_qb + n_ab].astype(BF16))
        return w_main, w_ab

    def mix_params(i, idx):
        return jnp.stack([mod[i, :, idx[0]], mod[i, :, idx[1]], mod[i, :, idx[2]]], axis=1)

    w_main, w_ab = in_weights(0)
    mp0 = jnp.stack([jnp.zeros_like(mod[0, :, 0]), mod[0, :, 0], mod[0, :, 1]], axis=1)
    h, ab = _row_stage(xs, None, mp0, jnp.stack([norm_gains[0, 0], norm_gains[0, 0]]), w_ab, nb, ncb, has_h=True)

    for i in range(depth):
        gains = norm_gains[i]
        p = _matmul(h, w_main, tm=tm, tn=_pick(n_main, (1024, 512, 256, 128)), name="in_proj")
        p3 = p.reshape(b, lt, n_main)
        g = _gates(ab, a_log[i], dt_bias[i])[:, :n_ab].reshape(b, lt, 4, H_A)
        g_col = g.transpose(0, 3, 1, 2)
        g_row = g.reshape(b, lt // CHUNK_A, CHUNK_A, 4, H_A).transpose(0, 4, 1, 3, 2)
        qkv = _conv_prep(p3, conv_w[i], nb, ncb, cb=_pick(H_A * DK_A, (512, 256, 128)))
        of, ob = _delta_scan(qkv, g_col, g_row, nb, ncb)
        o_a = _delta_out(of.reshape(m, w_a), ob.reshape(m, w_a), p, col_z, dn_norm_w[i],
                         cb=_pick(w_a, (512, 256, 128)))
        rf, rb = _ret_scan(p3, col_qb, col_kb, col_vb, cos, sin, lg, nb, ncb)
        o_b = _ret_out(rf.reshape(m, w_b), rb.reshape(m, w_b), p, col_gb, gn_w[i], gn_b[i],
                       cb=_pick(w_b, (512, 256)))
        tn_d = _pick(d, (1024, 512, 256, 128))
        merged = _branch_merge(o_a, w_br_a[i].astype(BF16), o_b, w_br_b[i].astype(BF16), p, col_ga, col_gate_b,
                               tm=tm, tn=_pick(d, (512, 256, 128)))
        y = _matmul(merged, w_out[i].astype(BF16), tm=tm, tn=tn_d, name="out_proj")
        j = i // 2
        is_moe = i % 2 == 1
        gn = jnp.stack([gains[1], gains[2]])
        if is_moe:
            n_exp = router.shape[2]
            w_r = jnp.zeros((d, AUX_W), BF16).at[:, :n_exp].set(router[j].astype(BF16))
            xs, h2, logits = _row_stage(xs, y, mix_params(i, (2, 3, 4)), gn, w_r, nb, ncb, has_h=True)
            combine = _router(logits, n_exp)
            f_e = moe_gate.shape[3]
            wg = moe_gate[j].transpose(1, 0, 2).reshape(d, n_exp * f_e).astype(BF16)
            wu = moe_up[j].transpose(1, 0, 2).reshape(d, n_exp * f_e).astype(BF16)
            wd = moe_down[j].reshape(n_exp * f_e, d).astype(BF16)
        else:
            xs, h2 = _row_stage(xs, y, mix_params(i, (2, 3, 4)), gn, None, nb, ncb, has_h=True)
            combine, f_e = None, 1
            wg, wu, wd = ffn_gate[j].astype(BF16), ffn_up[j].astype(BF16), ffn_down[j].astype(BF16)
        act = _ffn_up(h2, wg, wu, combine, f_e, tm=tm, tn=_pick(f_e if is_moe else wg.shape[1], (512, 256, 128)))
        y2 = _matmul(act, wd, tm=tm, tn=_pick(d, (512, 256, 128)), name="ffn_down")
        if i + 1 < depth:
            w_main, w_ab = in_weights(i + 1)
            mp = jnp.stack([mod[i, :, 5], mod[i + 1, :, 0], mod[i + 1, :, 1]], axis=1)
            gn = jnp.stack([gains[3], norm_gains[i + 1, 0]])
            xs, h, ab = _row_stage(xs, y2, mp, gn, w_ab, nb, ncb, has_h=True)
        else:
            mp = jnp.stack([mod[i, :, 5], mod[i, :, 5], mod[i, :, 5]], axis=1)
            (xs,) = _row_stage(xs, y2, mp, jnp.stack([gains[3], gains[3]]), None, nb, ncb, has_h=False)

    return xs.reshape(b, lt, d)[:, ctx_len:, :]
```

```python
import functools

import jax
import jax.numpy as jnp
from jax import lax
from jax.experimental import pallas as pl
from jax.experimental.pallas import tpu as pltpu

GRID_W = 64
CONV_W = 5
H_A = 16
DK_A = 128
DV_A = 128
H_B = 8
DK_B = 256
DV_B = 256
TOP_K = 2
ROPE_BASE = 10000.0
EPS = 1e-6
GN_EPS = 1e-5

LANES = 128
ROW_BLOCK = 256
CHUNK_A = 64
DELTA_HEADS = 4
AUX_W = LANES
VMEM_LIMIT = 56 * 1024 * 1024

F32 = jnp.float32
BF16 = jnp.bfloat16


def _cparams(sem):
    return pltpu.CompilerParams(dimension_semantics=sem, vmem_limit_bytes=VMEM_LIMIT)


def _silu(t):
    return t * jax.nn.sigmoid(t)


def _dot(a, b):
    return jnp.dot(a, b, preferred_element_type=F32)


def _dot_nt(a, b):
    return lax.dot_general(a, b, (((1,), (1,)), ((), ())), preferred_element_type=F32)


def _dot_tn(a, b):
    return lax.dot_general(a, b, (((0,), (0,)), ((), ())), preferred_element_type=F32)


def _hilo_lanes(a):
    hi = a.astype(BF16)
    hif = hi.astype(F32)
    lane = lax.broadcasted_iota(jnp.int32, a.shape, 1)
    return jnp.where(lane < a.shape[1] // 2, hif, a - hif).astype(BF16), hi


def _dot3(a, b):
    mixed, hi = _hilo_lanes(a)
    bh = b.astype(BF16)
    bl = (b - bh.astype(F32)).astype(BF16)
    lhs = jnp.concatenate([mixed, hi], axis=1)
    rhs = jnp.concatenate([bh, bh, bl, jnp.zeros_like(bh)], axis=0)
    return _dot(lhs, rhs)


def _mod_kernel(cond_ref, wd_ref, wu_ref, b_ref, o_ref, t_ref):
    @pl.when(pl.program_id(1) == 0)
    def _():
        s = _silu(cond_ref[...])
        t_ref[...] = _dot(s.astype(BF16), wd_ref[0].astype(BF16))

    o_ref[0] = _dot(t_ref[...].astype(BF16), wu_ref[0].astype(BF16)) + b_ref[0]


def _modulation(cond, w_down, w_up, bias):
    depth, d, r = w_down.shape
    n = w_up.shape[2]
    tn = _pick(n, (2048, 1024, 512, 256, 128))
    return pl.pallas_call(
        _mod_kernel,
        out_shape=jax.ShapeDtypeStruct((depth, 8, n), F32),
        grid=(depth, n // tn),
        in_specs=[
            pl.BlockSpec((8, d), lambda l, j: (0, 0)),
            pl.BlockSpec((1, d, r), lambda l, j: (l, 0, 0)),
            pl.BlockSpec((1, r, tn), lambda l, j: (l, 0, j)),
            pl.BlockSpec((1, 1, tn), lambda l, j: (l, 0, j)),
        ],
        out_specs=pl.BlockSpec((1, 8, tn), lambda l, j: (l, 0, j)),
        scratch_shapes=[pltpu.VMEM((8, r), F32)],
        compiler_params=_cparams(("arbitrary", "arbitrary")),
        name="modulation",
    )(cond, w_down, w_up, bias.reshape(depth, 1, n))


def _rms(t, gain):
    return t * lax.rsqrt(jnp.mean(t * t, axis=-1, keepdims=True) + EPS) * gain


def _row_kernel(*refs, has_y, has_h, has_aux):
    it = iter(refs)
    x_ref = next(it)
    y_ref = next(it) if has_y else None
    mp_ref = next(it)
    gn_ref = next(it)
    waux_ref = next(it) if has_aux else None
    xo_ref = next(it) if has_y else None
    h_ref = next(it) if has_h else None
    aux_ref = next(it) if has_aux else None

    x = x_ref[...]
    if has_y:
        x = x + mp_ref[0, 0:1, :] * _rms(y_ref[...].astype(F32), gn_ref[0:1, :])
        xo_ref[...] = x
    if has_h:
        h = _rms(x, gn_ref[1:2, :]) * (1.0 + mp_ref[0, 2:3, :]) + mp_ref[0, 1:2, :]
        hb = h.astype(BF16)
        h_ref[...] = hb
        if has_aux:
            aux_ref[...] = _dot(hb, waux_ref[...])


def _row_stage(x, y, mp, gn, waux, nb, ncb, *, has_h):
    m, d = x.shape
    has_y = y is not None
    has_aux = waux is not None
    n_ctx_rows = mp.shape[0] - 1

    def sel(i):
        return jnp.where(i % nb < ncb, n_ctx_rows, i // nb)

    row = pl.BlockSpec((ROW_BLOCK, d), lambda i: (i, 0))
    in_specs = [row]
    args = [x]
    if has_y:
        in_specs.append(row)
        args.append(y)
    in_specs += [pl.BlockSpec((1, 3, d), lambda i: (sel(i), 0, 0)), pl.BlockSpec((2, d), lambda i: (0, 0))]
    args += [mp, gn]
    if has_aux:
        in_specs.append(pl.BlockSpec((d, AUX_W), lambda i: (0, 0)))
        args.append(waux)
    out_shape, out_specs = [], []
    if has_y:
        out_shape.append(jax.ShapeDtypeStruct((m, d), F32))
        out_specs.append(row)
    if has_h:
        out_shape.append(jax.ShapeDtypeStruct((m, d), BF16))
        out_specs.append(row)
    if has_aux:
        out_shape.append(jax.ShapeDtypeStruct((m, AUX_W), F32))
        out_specs.append(pl.BlockSpec((ROW_BLOCK, AUX_W), lambda i: (i, 0)))
    outs = pl.pallas_call(
        functools.partial(_row_kernel, has_y=has_y, has_h=has_h, has_aux=has_aux),
        out_shape=out_shape,
        grid=(m // ROW_BLOCK,),
        in_specs=in_specs,
        out_specs=out_specs,
        compiler_params=_cparams(("parallel",)),
        name="row_stage",
    )(*args)
    return list(outs)


def _mm_kernel(a_ref, w_ref, o_ref):
    o_ref[...] = _dot(a_ref[...], w_ref[...]).astype(o_ref.dtype)


def _pick(n, pref):
    for t in pref:
        if n % t == 0:
            return t
    return n


def _matmul(a, w, *, tm, tn, out_dtype=BF16, name="matmul"):
    m, k = a.shape
    n = w.shape[1]
    return pl.pallas_call(
        _mm_kernel,
        out_shape=jax.ShapeDtypeStruct((m, n), out_dtype),
        grid=(m // tm, n // tn),
        in_specs=[pl.BlockSpec((tm, k), lambda i, j: (i, 0)), pl.BlockSpec((k, tn), lambda i, j: (0, j))],
        out_specs=pl.BlockSpec((tm, tn), lambda i, j: (i, j)),
        compiler_params=_cparams(("parallel", "parallel")),
        name=name,
    )(a, w)


def _merge_kernel(oa_ref, wa_ref, ob_ref, wb_ref, ga_ref, gb_ref, o_ref):
    ya = _dot(oa_ref[...], wa_ref[...])
    yb = _dot(ob_ref[...], wb_ref[...])
    ga = jax.nn.sigmoid(ga_ref[...].astype(F32))
    gb = jax.nn.sigmoid(gb_ref[...].astype(F32))
    o_ref[...] = (ga * ya + gb * yb).astype(o_ref.dtype)


def _branch_merge(o_a, w_a, o_b, w_b, p, col_ga, col_gb, *, tm, tn):
    m, ka = o_a.shape
    kb = o_b.shape[1]
    n = w_a.shape[1]
    ja, jb = col_ga // tn, col_gb // tn
    return pl.pallas_call(
        _merge_kernel,
        out_shape=jax.ShapeDtypeStruct((m, n), BF16),
        grid=(m // tm, n // tn),
        in_specs=[
            pl.BlockSpec((tm, ka), lambda i, j: (i, 0)),
            pl.BlockSpec((ka, tn), lambda i, j: (0, j)),
            pl.BlockSpec((tm, kb), lambda i, j: (i, 0)),
            pl.BlockSpec((kb, tn), lambda i, j: (0, j)),
            pl.BlockSpec((tm, tn), lambda i, j: (i, ja + j)),
            pl.BlockSpec((tm, tn), lambda i, j: (i, jb + j)),
        ],
        out_specs=pl.BlockSpec((tm, tn), lambda i, j: (i, j)),
        compiler_params=_cparams(("parallel", "parallel")),
        name="branch_merge",
    )(o_a, w_a, o_b, w_b, p, p)


def _ffn_up_kernel(*refs, tn, f_expert, has_combine):
    if has_combine:
        h_ref, wg_ref, wu_ref, cmb_ref, o_ref = refs
    else:
        h_ref, wg_ref, wu_ref, o_ref = refs
    h = h_ref[...]
    act = _silu(_dot(h, wg_ref[...])) * _dot(h, wu_ref[...])
    if has_combine:
        e = (pl.program_id(1) * tn) // f_expert
        cmb = cmb_ref[...]
        lane = lax.broadcasted_iota(jnp.int32, cmb.shape, 1)
        act = act * jnp.sum(jnp.where(lane == e, cmb, 0.0), axis=-1, keepdims=True)
    o_ref[...] = act.astype(o_ref.dtype)


def _ffn_up(h, w_gate, w_up, combine, f_expert, *, tm, tn):
    m, k = h.shape
    n = w_gate.shape[1]
    has_combine = combine is not None
    in_specs = [
        pl.BlockSpec((tm, k), lambda i, j: (i, 0)),
        pl.BlockSpec((k, tn), lambda i, j: (0, j)),
        pl.BlockSpec((k, tn), lambda i, j: (0, j)),
    ]
    args = [h, w_gate, w_up]
    if has_combine:
        in_specs.append(pl.BlockSpec((tm, AUX_W), lambda i, j: (i, 0)))
        args.append(combine)
    return pl.pallas_call(
        functools.partial(_ffn_up_kernel, tn=tn, f_expert=f_expert, has_combine=has_combine),
        out_shape=jax.ShapeDtypeStruct((m, n), BF16),
        grid=(m // tm, n // tn),
        in_specs=in_specs,
        out_specs=pl.BlockSpec((tm, tn), lambda i, j: (i, j)),
        compiler_params=_cparams(("parallel", "parallel")),
        name="ffn_up",
    )(*args)


def _router_kernel(lg_ref, o_ref, *, n_experts):
    lg = lg_ref[...]
    lane = lax.broadcasted_iota(jnp.int32, lg.shape, 1)
    neg = jnp.float32(-jnp.inf)
    valid = lane < n_experts
    v = jnp.where(valid, lg, neg)
    m1 = jnp.max(v, axis=-1, keepdims=True)
    i1 = jnp.min(jnp.where(v == m1, lane, AUX_W), axis=-1, keepdims=True)
    v2 = jnp.where(lane == i1, neg, v)
    m2 = jnp.max(v2, axis=-1, keepdims=True)
    i2 = jnp.min(jnp.where(v2 == m2, lane, AUX_W), axis=-1, keepdims=True)
    e2 = jnp.exp(m2 - m1)
    p1 = 1.0 / (1.0 + e2)
    p2 = e2 / (1.0 + e2)
    o_ref[...] = jnp.where(lane == i1, p1, 0.0) + jnp.where(lane == i2, p2, 0.0)


def _router(logits, n_experts):
    m = logits.shape[0]
    blk = pl.BlockSpec((ROW_BLOCK, AUX_W), lambda i: (i, 0))
    return pl.pallas_call(
        functools.partial(_router_kernel, n_experts=n_experts),
        out_shape=jax.ShapeDtypeStruct((m, AUX_W), F32),
        grid=(m // ROW_BLOCK,),
        in_specs=[blk],
        out_specs=blk,
        compiler_params=_cparams(("parallel",)),
        name="router_top2",
    )(logits)


def _gates_kernel(ab_ref, par_ref, o_ref):
    x = ab_ref[...]
    rows, width = x.shape
    lane = lax.broadcasted_iota(jnp.int32, x.shape, 1)
    pos = lax.broadcasted_iota(jnp.int32, x.shape, 0) % CHUNK_A
    beta = jax.nn.sigmoid(x)
    z = x + par_ref[1:2, :]
    softplus = jnp.maximum(z, 0.0) + jnp.log1p(jnp.exp(-jnp.abs(z)))
    g = -jnp.exp(par_ref[0:1, :]) * softplus
    pre = g
    suf = g
    s = 1
    while s < CHUNK_A:
        pre = pre + jnp.where(pos >= s, pltpu.roll(pre, s, 0), 0.0)
        suf = suf + jnp.where(pos < CHUNK_A - s, pltpu.roll(suf, rows - s, 0), 0.0)
        s *= 2
    o_ref[...] = jnp.where(lane < 2 * H_A, beta, jnp.where(lane < 3 * H_A, pre, suf))


def _gates(ab, a_log, dt_bias):
    m = ab.shape[0]
    par = jnp.zeros((2, AUX_W), F32)
    par = par.at[0, 2 * H_A:4 * H_A].set(a_log.reshape(-1).astype(F32))
    par = par.at[1, 2 * H_A:4 * H_A].set(dt_bias.reshape(-1).astype(F32))
    blk = pl.BlockSpec((ROW_BLOCK, AUX_W), lambda i: (i, 0))
    return pl.pallas_call(
        _gates_kernel,
        out_shape=jax.ShapeDtypeStruct((m, AUX_W), F32),
        grid=(m // ROW_BLOCK,),
        in_specs=[blk, pl.BlockSpec((2, AUX_W), lambda i: (0, 0))],
        out_specs=blk,
        compiler_params=_cparams(("parallel",)),
        name="delta_gates",
    )(ab, par)


_HALO = 16


def _conv_kernel(prev_ref, cur_ref, next_ref, w_ref, o_ref, *, cb, nb, ncb, n_qk_blocks, n_q_blocks):
    t = pl.program_id(1)
    c = pl.program_id(2)
    prev_ok = jnp.logical_and(t != 0, t != ncb)
    next_ok = jnp.logical_and(t != ncb - 1, t != nb - 1)
    prev = jnp.where(prev_ok, prev_ref[0].astype(F32), 0.0)
    nxt = jnp.where(next_ok, next_ref[0].astype(F32), 0.0)
    x = jnp.concatenate([prev, cur_ref[0].astype(F32), nxt], axis=0)
    rows = x.shape[0]
    w = w_ref[...]
    y = None
    for d in range(CONV_W):
        shift = (CONV_W // 2 - d) % rows
        xs = x if shift == 0 else pltpu.roll(x, shift, 0)
        term = xs[_HALO:_HALO + ROW_BLOCK] * w[d:d + 1, :]
        y = term if y is None else y + term
    y = _silu(y)
    is_qk = c < n_qk_blocks
    q_scale = jnp.where(c < n_q_blocks, jnp.float32(DK_A ** -0.5), jnp.float32(1.0))
    for hh in range(cb // DK_A):
        s = y[:, hh * DK_A:(hh + 1) * DK_A]
        inv = lax.rsqrt(jnp.sum(s * s, axis=-1, keepdims=True) + EPS) * q_scale
        o_ref[0, :, hh * DK_A:(hh + 1) * DK_A] = (s * jnp.where(is_qk, inv, 1.0)).astype(o_ref.dtype)


def _conv_prep(p3, conv_w, nb, ncb, cb=512):
    b, lt, _ = p3.shape
    wc = conv_w.shape[1]
    per = ROW_BLOCK // _HALO
    n_halo = lt // _HALO
    kern = functools.partial(_conv_kernel, cb=cb, nb=nb, ncb=ncb,
                             n_qk_blocks=2 * H_A * DK_A // cb, n_q_blocks=H_A * DK_A // cb)
    return pl.pallas_call(
        kern,
        out_shape=jax.ShapeDtypeStruct((b, lt, wc), BF16),
        grid=(b, nb, wc // cb),
        in_specs=[
            pl.BlockSpec((1, _HALO, cb), lambda i, t, c: (i, jnp.maximum(t * per - 1, 0), c)),
            pl.BlockSpec((1, ROW_BLOCK, cb), lambda i, t, c: (i, t, c)),
            pl.BlockSpec((1, _HALO, cb), lambda i, t, c: (i, jnp.minimum((t + 1) * per, n_halo - 1), c)),
            pl.BlockSpec((CONV_W, cb), lambda i, t, c: (0, c)),
        ],
        out_specs=pl.BlockSpec((1, ROW_BLOCK, cb), lambda i, t, c: (i, t, c)),
        compiler_params=_cparams(("parallel", "parallel", "parallel")),
        name="conv_prep",
    )(p3, p3, p3, conv_w)


def _delta_prepare(items, wq_ref, u_ref, at_ref, kd_ref, ge_ref):
    c = CHUNK_A
    row = lax.broadcasted_iota(jnp.int32, (c, 2 * c), 0)
    col = lax.broadcasted_iota(jnp.int32, (c, 2 * c), 1) % c
    eye = (row == col).astype(F32)
    decay, t_inv, pw = [], [], []
    for (_, reverse, q, k, v, beta, gc, gc_row) in items:
        incl = (row <= col) if reverse else (row >= col)
        decay.append(jnp.exp(jnp.where(incl, gc - gc_row, -1e30)))
    k2 = [jnp.concatenate([it[3], it[3]], axis=0) for it in items]
    kk = [_dot_nt(it[3], kd) for it, kd in zip(items, k2)]
    qk = [_dot_nt(it[2], kd) for it, kd in zip(items, k2)]
    for i, (_, reverse, q, k, v, beta, gc, gc_row) in enumerate(items):
        strict = (row < col) if reverse else (row > col)
        n = jnp.where(strict, beta * kk[i] * decay[i], 0.0)
        pw.append(n)
        t_inv.append(eye - n)
    pw = [_dot3(p, p) for p in pw]
    span = 4
    while span < c:
        prod = [_dot3(jnp.concatenate([p, t], axis=0), p) for p, t in zip(pw, t_inv)]
        pw = [r[0:c] for r in prod]
        t_inv = [t + r[c:2 * c] for t, r in zip(t_inv, prod)]
        span *= 2
    t_inv = [t + _dot3(t, p) for t, p in zip(t_inv, pw)]
    rhs, e_gc = [], []
    for (_, reverse, q, k, v, beta, gc, gc_row) in items:
        e = jnp.exp(gc)
        e_gc.append(e)
        r = jnp.concatenate([v.astype(F32) * beta, k.astype(F32) * (beta * e)], axis=-1).astype(BF16)
        rhs.append(jnp.concatenate([r, r], axis=0))
    uw = [_dot(_hilo_lanes(t)[0], r) for t, r in zip(t_inv, rhs)]
    for i, (slot, reverse, q, k, v, beta, gc, gc_row) in enumerate(items):
        last = 0 if reverse else c - 1
        g_last = gc[last:last + 1, :]
        u_ref[slot] = uw[i][:, :DV_A]
        wq_ref[slot, 0:c, :] = uw[i][:, DV_A:].astype(BF16)
        wq_ref[slot, c:2 * c, :] = (q.astype(F32) * e_gc[i]).astype(BF16)
        at_ref[slot] = (qk[i] * decay[i])[:, 0:c].astype(BF16)
        kd_ref[slot] = (k.astype(F32) * jnp.exp(g_last - gc)).astype(BF16)
        ge_ref[slot] = jnp.broadcast_to(jnp.exp(g_last), ge_ref.shape[1:])


def _delta_kernel(qf_ref, kf_ref, vf_ref, gcf_ref, grf_ref, qb_ref, kb_ref, vb_ref, gcb_ref, grb_ref,
                  of_ref, ob_ref, s_ref, wq_ref, u_ref, at_ref, kd_ref, ge_ref, *, heads):
    @pl.when(pl.program_id(2) == 0)
    def _():
        s_ref[...] = jnp.zeros_like(s_ref)

    c = CHUNK_A
    nchunks = ROW_BLOCK // c
    fwd = (qf_ref, kf_ref, vf_ref, gcf_ref, grf_ref, of_ref)
    bwd = (qb_ref, kb_ref, vb_ref, gcb_ref, grb_ref, ob_ref)

    def slot(hh, reverse, cc):
        return (hh * 2 + int(reverse)) * nchunks + cc

    for hh in range(heads):
        items = []
        for reverse in (False, True):
            q_ref, k_ref, v_ref, gc_ref, gr_ref, _ = bwd if reverse else fwd
            bcol, gcol = (1, 3) if reverse else (0, 2)
            hs = slice(hh * DK_A, (hh + 1) * DK_A)
            for cc in range(nchunks):
                rows = slice(cc * c, (cc + 1) * c)
                gcols = gc_ref[0, hh, rows, :]
                items.append((slot(hh, reverse, cc), reverse, q_ref[0, rows, hs], k_ref[0, rows, hs],
                              v_ref[0, rows, hs], gcols[:, bcol:bcol + 1], gcols[:, gcol:gcol + 1],
                              gr_ref[0, hh, cc, gcol:gcol + 1, :]))
        _delta_prepare(items, wq_ref, u_ref, at_ref, kd_ref, ge_ref)

    chains = [(hh, reverse) for hh in range(heads) for reverse in (False, True)]
    states = [s_ref[hh * 2 + int(reverse)] for hh, reverse in chains]
    for ci in range(nchunks):
        slots = [slot(hh, reverse, nchunks - 1 - ci if reverse else ci) for hh, reverse in chains]
        sbs = [s.astype(BF16) for s in states]
        ws = [_dot(wq_ref[sl], sb) for sl, sb in zip(slots, sbs)]
        vns = [(u_ref[sl] - r[0:c]).astype(BF16) for sl, r in zip(slots, ws)]
        intra = [_dot(at_ref[sl], vn) for sl, vn in zip(slots, vns)]
        upd = [_dot_tn(kd_ref[sl], vn) for sl, vn in zip(slots, vns)]
        for j, (hh, reverse) in enumerate(chains):
            cc = nchunks - 1 - ci if reverse else ci
            o_ref = bwd[5] if reverse else fwd[5]
            o_ref[0, cc * c:(cc + 1) * c, hh * DV_A:(hh + 1) * DV_A] = ws[j][c:2 * c] + intra[j]
            states[j] = states[j] * ge_ref[slots[j], 0:1, :] + upd[j]
    for j, (hh, reverse) in enumerate(chains):
        s_ref[hh * 2 + int(reverse)] = states[j]


def _bwd_block(t, nb, ncb):
    return jnp.where(t < ncb, ncb - 1 - t, nb - 1 - (t - ncb))


def _delta_scan(qkv, g_col, g_row, nb, ncb):
    b, lt, _ = qkv.shape
    nch = ROW_BLOCK // CHUNK_A
    hg = _pick(H_A, (DELTA_HEADS, 2, 1))
    ngrp = H_A // hg
    n_slots = hg * 2 * nch

    def blk(rev, t):
        return _bwd_block(t, nb, ncb) if rev else t

    def tok(grp0, rev):
        return pl.BlockSpec((1, ROW_BLOCK, hg * DK_A), lambda i, h, t: (i, blk(rev, t), grp0 + h))

    def gcol(rev):
        return pl.BlockSpec((1, hg, ROW_BLOCK, 4), lambda i, h, t: (i, h, blk(rev, t), 0))

    def grow(rev):
        return pl.BlockSpec((1, hg, nch, 4, 2 * CHUNK_A), lambda i, h, t: (i, h, blk(rev, t), 0, 0))

    in_specs, args = [], []
    for rev in (False, True):
        in_specs += [tok(0, rev), tok(ngrp, rev), tok(2 * ngrp, rev), gcol(rev), grow(rev)]
        args += [qkv, qkv, qkv, g_col, g_row]
    o_shape = jax.ShapeDtypeStruct((b, lt, H_A * DV_A), F32)
    return pl.pallas_call(
        functools.partial(_delta_kernel, heads=hg),
        out_shape=[o_shape, o_shape],
        grid=(b, ngrp, nb),
        in_specs=in_specs,
        out_specs=[tok(0, False), tok(0, True)],
        scratch_shapes=[
            pltpu.VMEM((hg * 2, DK_A, DV_A), F32),
            pltpu.VMEM((n_slots, 2 * CHUNK_A, DK_A), BF16),
            pltpu.VMEM((n_slots, CHUNK_A, DV_A), F32),
            pltpu.VMEM((n_slots, CHUNK_A, CHUNK_A), BF16),
            pltpu.VMEM((n_slots, CHUNK_A, DK_A), BF16),
            pltpu.VMEM((n_slots, 8, LANES), F32),
        ],
        compiler_params=_cparams(("parallel", "parallel", "arbitrary")),
        name="delta_scan",
    )(*args)


def _delta_out_kernel(of_ref, ob_ref, z_ref, w_ref, o_ref, *, cb):
    o = of_ref[...] + ob_ref[...]
    z = z_ref[...].astype(F32)
    w = w_ref[...]
    for hh in range(cb // DV_A):
        sl = slice(hh * DV_A, (hh + 1) * DV_A)
        s = o[:, sl]
        y = s * lax.rsqrt(jnp.mean(s * s, axis=-1, keepdims=True) + EPS) * w
        o_ref[:, sl] = (y * _silu(z[:, sl])).astype(o_ref.dtype)


def _delta_out(o_f, o_b, p, col_z, dn_norm_w, cb=512):
    m, wa = o_f.shape
    blk = pl.BlockSpec((ROW_BLOCK, cb), lambda i, c: (i, c))
    j0 = col_z // cb
    return pl.pallas_call(
        functools.partial(_delta_out_kernel, cb=cb),
        out_shape=jax.ShapeDtypeStruct((m, wa), BF16),
        grid=(m // ROW_BLOCK, wa // cb),
        in_specs=[blk, blk, pl.BlockSpec((ROW_BLOCK, cb), lambda i, c: (i, j0 + c)),
                  pl.BlockSpec((1, DV_A), lambda i, c: (0, 0))],
        out_specs=blk,
        compiler_params=_cparams(("parallel", "parallel")),
        name="delta_out",
    )(o_f, o_b, p, dn_norm_w.reshape(1, DV_A).astype(F32))


def _rotate(t, cos, sin):
    half = t.shape[-1] // 2
    t1, t2 = t[:, :half], t[:, half:]
    return jnp.concatenate([t1 * cos - t2 * sin, t2 * cos + t1 * sin], axis=-1)


def _ret_dir(q, k, v, lg, s_ref, reverse):
    c = q.shape[0]
    row = lax.broadcasted_iota(jnp.int32, (c, c), 0)
    col = lax.broadcasted_iota(jnp.int32, (c, c), 1)
    dist = (col - row) if reverse else (row - col)
    decay = jnp.exp(jnp.where(dist >= 0, dist.astype(F32) * lg, -1e30))
    pos = lax.broadcasted_iota(jnp.int32, (c, 1), 0)
    pos = (c - 1 - pos) if reverse else pos
    posf = pos.astype(F32)
    qb = q.astype(BF16)
    kb = k.astype(BF16)
    s = s_ref[...]
    o = _dot((_dot_nt(qb, kb) * decay).astype(BF16), v)
    o = o + _dot((q * jnp.exp((posf + 1.0) * lg)).astype(BF16), s.astype(BF16))
    k_dec = (k * jnp.exp((c - 1.0 - posf) * lg)).astype(BF16)
    s_ref[...] = s * jnp.exp(c * lg) + _dot_tn(k_dec, v)
    return o


def _ret_kernel(qf_ref, kf_ref, vf_ref, cf_ref, sf_ref, qb_ref, kb_ref, vb_ref, cb_ref, sb_ref, lg_ref,
                of_ref, ob_ref, stf_ref, stb_ref):
    @pl.when(pl.program_id(2) == 0)
    def _():
        stf_ref[...] = jnp.zeros_like(stf_ref)
        stb_ref[...] = jnp.zeros_like(stb_ref)

    k_scale = DK_B ** -0.5
    for reverse in (False, True):
        q_ref, k_ref, v_ref, c_ref, s_ref, o_ref, st_ref = (
            (qb_ref, kb_ref, vb_ref, cb_ref, sb_ref, ob_ref, stb_ref) if reverse else
            (qf_ref, kf_ref, vf_ref, cf_ref, sf_ref, of_ref, stf_ref))
        cos, sin = c_ref[...], s_ref[...]
        q = _rotate(q_ref[0].astype(F32), cos, sin)
        k = _rotate(k_ref[0].astype(F32) * k_scale, cos, sin)
        lg = lg_ref[0, 1 if reverse else 0, 0:1, 0:1]
        o_ref[0] = _ret_dir(q, k, v_ref[0], lg, st_ref, reverse)


def _ret_scan(p3, col_q, col_k, col_v, cos, sin, lg, nb, ncb):
    b, lt, _ = p3.shape
    jq, jk, jv = col_q // DK_B, col_k // DK_B, col_v // DV_B

    def tok(j0, rev):
        if rev:
            return pl.BlockSpec((1, ROW_BLOCK, DK_B), lambda i, h, t: (i, _bwd_block(t, nb, ncb), j0 + h))
        return pl.BlockSpec((1, ROW_BLOCK, DK_B), lambda i, h, t: (i, t, j0 + h))

    def ang(rev):
        if rev:
            return pl.BlockSpec((ROW_BLOCK, DK_B // 2), lambda i, h, t: (_bwd_block(t, nb, ncb), 0))
        return pl.BlockSpec((ROW_BLOCK, DK_B // 2), lambda i, h, t: (t, 0))

    in_specs, args = [], []
    for rev in (False, True):
        in_specs += [tok(jq, rev), tok(jk, rev), tok(jv, rev), ang(rev), ang(rev)]
        args += [p3, p3, p3, cos, sin]
    in_specs.append(pl.BlockSpec((1, 2, 8, LANES), lambda i, h, t: (h, 0, 0, 0)))
    args.append(lg)
    o_shape = jax.ShapeDtypeStruct((b, lt, H_B * DV_B), F32)
    return pl.pallas_call(
        _ret_kernel,
        out_shape=[o_shape, o_shape],
        grid=(b, H_B, nb),
        in_specs=in_specs,
        out_specs=[tok(0, False), tok(0, True)],
        scratch_shapes=[pltpu.VMEM((DK_B, DV_B), F32), pltpu.VMEM((DK_B, DV_B), F32)],
        compiler_params=_cparams(("parallel", "parallel", "arbitrary")),
        name="retention_scan",
    )(*args)


def _ret_out_kernel(of_ref, ob_ref, g_ref, w_ref, b_ref, o_ref, *, cb):
    o = of_ref[...] + ob_ref[...]
    g = g_ref[...].astype(F32)
    for hh in range(cb // DV_B):
        sl = slice(hh * DV_B, (hh + 1) * DV_B)
        s = o[:, sl]
        d = s - jnp.mean(s, axis=-1, keepdims=True)
        y = d * lax.rsqrt(jnp.mean(d * d, axis=-1, keepdims=True) + GN_EPS)
        o_ref[:, sl] = ((y * w_ref[:, sl] + b_ref[:, sl]) * _silu(g[:, sl])).astype(o_ref.dtype)


def _ret_out(o_f, o_b, p, col_g, gn_w, gn_b, cb=512):
    m, wb = o_f.shape
    blk = pl.BlockSpec((ROW_BLOCK, cb), lambda i, c: (i, c))
    par = pl.BlockSpec((1, cb), lambda i, c: (0, c))
    j0 = col_g // cb
    return pl.pallas_call(
        functools.partial(_ret_out_kernel, cb=cb),
        out_shape=jax.ShapeDtypeStruct((m, wb), BF16),
        grid=(m // ROW_BLOCK, wb // cb),
        in_specs=[blk, blk, pl.BlockSpec((ROW_BLOCK, cb), lambda i, c: (i, j0 + c)), par, par],
        out_specs=blk,
        compiler_params=_cparams(("parallel", "parallel")),
        name="retention_out",
    )(o_f, o_b, p, gn_w.reshape(1, wb).astype(F32), gn_b.reshape(1, wb).astype(F32))


def _rope_tables(ctx_len, seq):
    n_freq = DK_B // 4
    freqs = ROPE_BASE ** (-jnp.arange(n_freq, dtype=F32) / n_freq)
    rows = seq // GRID_W
    r = jnp.repeat(jnp.arange(rows, dtype=F32), GRID_W)
    col = jnp.tile(jnp.arange(GRID_W, dtype=F32), rows)
    ang = jnp.concatenate([r[:, None] * freqs, col[:, None] * freqs], axis=-1)
    ang = jnp.concatenate([jnp.zeros((ctx_len, DK_B // 2), F32), ang], axis=0)
    return jnp.cos(ang), jnp.sin(ang)


def _retention_log_decays():
    h = jnp.arange(H_B, dtype=F32)
    fwd = jnp.log1p(-jnp.exp2(-5.0 - h))
    bwd = jnp.log1p(-jnp.exp2(-5.5 - h))
    lg = jnp.stack([fwd, bwd], axis=1)
    return jnp.broadcast_to(lg[:, :, None, None], (H_B, 2, 8, LANES))


def kernel(x, c, ctx, c_ctx, mod_down, mod_up, mod_bias, norm_gains, w_in, conv_w, a_log, dt_bias, dn_norm_w,
           gn_w, gn_b, w_br_a, w_br_b, w_out, ffn_gate, ffn_up, ffn_down, router, moe_gate, moe_up, moe_down):
    b, seq, d = x.shape
    ctx_len = ctx.shape[1]
    depth = w_in.shape[0]
    lt = ctx_len + seq
    assert ctx_len % ROW_BLOCK == 0 and seq % ROW_BLOCK == 0 and seq % GRID_W == 0 and b + 1 <= 8
    nb, ncb = lt // ROW_BLOCK, ctx_len // ROW_BLOCK
    m = b * lt
    w_conv = 2 * H_A * DK_A + H_A * DV_A
    w_a, w_b = H_A * DV_A, H_B * DV_B
    n_ab = 4 * H_A
    col_z = w_conv
    col_qb = col_z + w_a
    col_kb = col_qb + H_B * DK_B
    col_vb = col_kb + H_B * DK_B
    col_gb = col_vb + w_b
    col_ga = col_gb + w_b
    col_gate_b = col_ga + d
    n_main = col_gate_b + d

    tm = _pick(m, (768, 512, 256))

    cond = jnp.zeros((8, d), F32).at[:b].set(c).at[b].set(c_ctx)
    mod = _modulation(cond, mod_down, mod_up, mod_bias)[:, :b + 1].reshape(depth, b + 1, 6, d)

    cos, sin = _rope_tables(ctx_len, seq)
    lg = _retention_log_decays()

    xs = jnp.concatenate([ctx, x], axis=1).reshape(m, d)

    def in_weights(i):
        w = w_in[i]
        w_main = jnp.concatenate([w[:, :col_qb], w[:, col_qb + n_ab:]], axis=1).astype(BF16)
        w_ab = jnp.zeros((d, AUX_W), BF16).at[:, :n_ab].set(w[:, col_qb:col_qb + n_ab].astype(BF16))
        return w_main, w_ab

    def mix_params(i, idx):
        return jnp.stack([mod[i, :, idx[0]], mod[i, :, idx[1]], mod[i, :, idx[2]]], axis=1)

    w_main, w_ab = in_weights(0)
    mp0 = jnp.stack([jnp.zeros_like(mod[0, :, 0]), mod[0, :, 0], mod[0, :, 1]], axis=1)
    h, ab = _row_stage(xs, None, mp0, jnp.stack([norm_gains[0, 0], norm_gains[0, 0]]), w_ab, nb, ncb, has_h=True)

    for i in range(depth):
        gains = norm_gains[i]
        p = _matmul(h, w_main, tm=tm, tn=_pick(n_main, (1024, 512, 256, 128)), name="in_proj")
        p3 = p.reshape(b, lt, n_main)
        g = _gates(ab, a_log[i], dt_bias[i])[:, :n_ab].reshape(b, lt, 4, H_A)
        g_col = g.transpose(0, 3, 1, 2)
        g_row = g.reshape(b, lt // CHUNK_A, CHUNK_A, 4, H_A).transpose(0, 4, 1, 3, 2)
        g_row = jnp.concatenate([g_row, g_row], axis=-1)
        qkv = _conv_prep(p3, conv_w[i], nb, ncb, cb=_pick(H_A * DK_A, (512, 256, 128)))
        of, ob = _delta_scan(qkv, g_col, g_row, nb, ncb)
        o_a = _delta_out(of.reshape(m, w_a), ob.reshape(m, w_a), p, col_z, dn_norm_w[i],
                         cb=_pick(w_a, (512, 256, 128)))
        rf, rb = _ret_scan(p3, col_qb, col_kb, col_vb, cos, sin, lg, nb, ncb)
        o_b = _ret_out(rf.reshape(m, w_b), rb.reshape(m, w_b), p, col_gb, gn_w[i], gn_b[i],
                       cb=_pick(w_b, (512, 256)))
        tn_d = _pick(d, (1024, 512, 256, 128))
        merged = _branch_merge(o_a, w_br_a[i].astype(BF16), o_b, w_br_b[i].astype(BF16), p, col_ga, col_gate_b,
                               tm=tm, tn=_pick(d, (512, 256, 128)))
        y = _matmul(merged, w_out[i].astype(BF16), tm=tm, tn=tn_d, name="out_proj")
        j = i // 2
        is_moe = i % 2 == 1
        gn = jnp.stack([gains[1], gains[2]])
        if is_moe:
            n_exp = router.shape[2]
            w_r = jnp.zeros((d, AUX_W), BF16).at[:, :n_exp].set(router[j].astype(BF16))
            xs, h2, logits = _row_stage(xs, y, mix_params(i, (2, 3, 4)), gn, w_r, nb, ncb, has_h=True)
            combine = _router(logits, n_exp)
            f_e = moe_gate.shape[3]
            wg = moe_gate[j].transpose(1, 0, 2).reshape(d, n_exp * f_e).astype(BF16)
            wu = moe_up[j].transpose(1, 0, 2).reshape(d, n_exp * f_e).astype(BF16)
            wd = moe_down[j].reshape(n_exp * f_e, d).astype(BF16)
        else:
            xs, h2 = _row_stage(xs, y, mix_params(i, (2, 3, 4)), gn, None, nb, ncb, has_h=True)
            combine, f_e = None, 1
            wg, wu, wd = ffn_gate[j].astype(BF16), ffn_up[j].astype(BF16), ffn_down[j].astype(BF16)
        act = _ffn_up(h2, wg, wu, combine, f_e, tm=tm, tn=_pick(f_e if is_moe else wg.shape[1], (512, 256, 128)))
        y2 = _matmul(act, wd, tm=tm, tn=_pick(d, (512, 256, 128)), name="ffn_down")
        if i + 1 < depth:
            w_main, w_ab = in_weights(i + 1)
            mp = jnp.stack([mod[i, :, 5], mod[i + 1, :, 0], mod[i + 1, :, 1]], axis=1)
            gn = jnp.stack([gains[3], norm_gains[i + 1, 0]])
            xs, h, ab = _row_stage(xs, y2, mp, gn, w_ab, nb, ncb, has_h=True)
        else:
            mp = jnp.stack([mod[i, :, 5], mod[i, :, 5], mod[i, :, 5]], axis=1)
            (xs,) = _row_stage(xs, y2, mp, jnp.stack([gains[3], gains[3]]), None, nb, ncb, has_h=False)

    return xs.reshape(b, lt, d)[:, ctx_len:, :]
```

```python
import functools

import jax
import jax.numpy as jnp
from jax import lax
from jax.experimental import pallas as pl
from jax.experimental.pallas import tpu as pltpu

GRID_W = 64
CONV_W = 5
H_A = 16
DK_A = 128
DV_A = 128
H_B = 8
DK_B = 256
DV_B = 256
TOP_K = 2
ROPE_BASE = 10000.0
EPS = 1e-6
GN_EPS = 1e-5

LANES = 128
ROW_BLOCK = 256
CHUNK_A = 64
DELTA_HEADS = 4
DELTA_WAVE = 2
AUX_W = LANES
VMEM_LIMIT = 56 * 1024 * 1024

F32 = jnp.float32
BF16 = jnp.bfloat16


def _cparams(sem):
    return pltpu.CompilerParams(dimension_semantics=sem, vmem_limit_bytes=VMEM_LIMIT)


def _silu(t):
    return t * jax.nn.sigmoid(t)


def _dot(a, b):
    return jnp.dot(a, b, preferred_element_type=F32)


def _dot_nt(a, b):
    return lax.dot_general(a, b, (((1,), (1,)), ((), ())), preferred_element_type=F32)


def _dot_tn(a, b):
    return lax.dot_general(a, b, (((0,), (0,)), ((), ())), preferred_element_type=F32)


def _hilo_lanes(a):
    hif = a.astype(BF16).astype(F32)
    lane = lax.broadcasted_iota(jnp.int32, a.shape, 1)
    return jnp.where(lane < a.shape[1] // 2, hif, a - hif).astype(BF16)


def _dot3(a, b):
    bh = b.astype(BF16)
    bl = (b - bh.astype(F32)).astype(BF16)
    lhs = jnp.concatenate([_hilo_lanes(a), a.astype(BF16)], axis=1)
    rhs = jnp.concatenate([bh, bh, bl, jnp.zeros_like(bh)], axis=0)
    return _dot(lhs, rhs)


def _mod_kernel(cond_ref, wd_ref, wu_ref, b_ref, o_ref, t_ref):
    @pl.when(pl.program_id(1) == 0)
    def _():
        s = _silu(cond_ref[...])
        t_ref[...] = _dot(s.astype(BF16), wd_ref[0].astype(BF16))

    o_ref[0] = _dot(t_ref[...].astype(BF16), wu_ref[0].astype(BF16)) + b_ref[0]


def _modulation(cond, w_down, w_up, bias):
    depth, d, r = w_down.shape
    n = w_up.shape[2]
    tn = _pick(n, (2048, 1024, 512, 256, 128))
    return pl.pallas_call(
        _mod_kernel,
        out_shape=jax.ShapeDtypeStruct((depth, 8, n), F32),
        grid=(depth, n // tn),
        in_specs=[
            pl.BlockSpec((8, d), lambda l, j: (0, 0)),
            pl.BlockSpec((1, d, r), lambda l, j: (l, 0, 0)),
            pl.BlockSpec((1, r, tn), lambda l, j: (l, 0, j)),
            pl.BlockSpec((1, 1, tn), lambda l, j: (l, 0, j)),
        ],
        out_specs=pl.BlockSpec((1, 8, tn), lambda l, j: (l, 0, j)),
        scratch_shapes=[pltpu.VMEM((8, r), F32)],
        compiler_params=_cparams(("arbitrary", "arbitrary")),
        name="modulation",
    )(cond, w_down, w_up, bias.reshape(depth, 1, n))


def _rms(t, gain):
    return t * lax.rsqrt(jnp.mean(t * t, axis=-1, keepdims=True) + EPS) * gain


def _row_kernel(*refs, has_y, has_h, has_aux, nb, ncb, split_in):
    it = iter(refs)
    c_ref = next(it) if split_in else None
    x_ref = next(it)
    y_ref = next(it) if has_y else None
    mp_ref = next(it)
    gn_ref = next(it)
    waux_ref = next(it) if has_aux else None
    xo_ref = next(it) if (has_y or split_in) else None
    h_ref = next(it) if has_h else None
    aux_ref = next(it) if has_aux else None

    x = x_ref[...]
    if split_in:
        x = jnp.where(pl.program_id(0) % nb < ncb, c_ref[...], x)
    if has_y:
        x = x + mp_ref[0, 0:1, :] * _rms(y_ref[...].astype(F32), gn_ref[0:1, :])
    if xo_ref is not None:
        xo_ref[...] = x
    if has_h:
        h = _rms(x, gn_ref[1:2, :]) * (1.0 + mp_ref[0, 2:3, :]) + mp_ref[0, 1:2, :]
        hb = h.astype(BF16)
        h_ref[...] = hb
        if has_aux:
            aux_ref[...] = _dot(hb, waux_ref[...])


def _row_stage(x, y, mp, gn, waux, nb, ncb, *, has_h, ctx_rows=None, latent_only=False):
    d = x.shape[1]
    has_y = y is not None
    has_aux = waux is not None
    split_in = ctx_rows is not None
    n_ctx_rows = mp.shape[0] - 1
    nlb = nb - ncb
    m = x.shape[0] + ctx_rows.shape[0] if split_in else x.shape[0]
    steps_per_batch = nlb if latent_only else nb
    first = ncb if latent_only else 0

    def full(i):
        return (i // steps_per_batch) * nb + first + i % steps_per_batch

    def sel(i):
        return jnp.where(full(i) % nb < ncb, n_ctx_rows, full(i) // nb)

    row = pl.BlockSpec((ROW_BLOCK, d), lambda i: (full(i), 0))
    out_row = pl.BlockSpec((ROW_BLOCK, d), lambda i: (i, 0))
    in_specs, args = [], []
    if split_in:
        in_specs += [pl.BlockSpec((ROW_BLOCK, d), lambda i: ((i // nb) * ncb + jnp.minimum(i % nb, ncb - 1), 0)),
                     pl.BlockSpec((ROW_BLOCK, d), lambda i: ((i // nb) * nlb + jnp.maximum(i % nb - ncb, 0), 0))]
        args += [ctx_rows, x]
    else:
        in_specs.append(row)
        args.append(x)
    if has_y:
        in_specs.append(row)
        args.append(y)
    in_specs += [pl.BlockSpec((1, 3, d), lambda i: (sel(i), 0, 0)), pl.BlockSpec((2, d), lambda i: (0, 0))]
    args += [mp, gn]
    if has_aux:
        in_specs.append(pl.BlockSpec((d, AUX_W), lambda i: (0, 0)))
        args.append(waux)
    n_out_rows = (m // nb) * nlb if latent_only else m
    out_shape, out_specs = [], []
    if has_y or split_in:
        out_shape.append(jax.ShapeDtypeStruct((n_out_rows, d), F32))
        out_specs.append(out_row)
    if has_h:
        out_shape.append(jax.ShapeDtypeStruct((n_out_rows, d), BF16))
        out_specs.append(out_row)
    if has_aux:
        out_shape.append(jax.ShapeDtypeStruct((n_out_rows, AUX_W), F32))
        out_specs.append(pl.BlockSpec((ROW_BLOCK, AUX_W), lambda i: (i, 0)))
    outs = pl.pallas_call(
        functools.partial(_row_kernel, has_y=has_y, has_h=has_h, has_aux=has_aux, nb=nb, ncb=ncb,
                          split_in=split_in),
        out_shape=out_shape,
        grid=(n_out_rows // ROW_BLOCK,),
        in_specs=in_specs,
        out_specs=out_specs,
        compiler_params=_cparams(("parallel",)),
        name="row_stage",
    )(*args)
    return list(outs)


def _cast_kernel(w_ref, o_ref):
    o_ref[...] = w_ref[...].astype(o_ref.dtype)


def _to_bf16(w):
    w2 = w.reshape(-1, w.shape[-1])
    r, c = w2.shape
    tr = _pick(r, (512, 256, 128, 64, 32, 16))
    tc = _pick(c, (2048, 1024, 512, 256, 128))
    blk = pl.BlockSpec((tr, tc), lambda i, j: (i, j))
    out = pl.pallas_call(
        _cast_kernel,
        out_shape=jax.ShapeDtypeStruct((r, c), BF16),
        grid=(r // tr, c // tc),
        in_specs=[blk],
        out_specs=blk,
        compiler_params=_cparams(("parallel", "parallel")),
        name="weights_to_bf16",
    )(w2)
    return out.reshape(w.shape)


def _mm_kernel(a_ref, w_ref, o_ref):
    o_ref[...] = _dot(a_ref[...], w_ref[...]).astype(o_ref.dtype)


def _pick(n, pref):
    for t in pref:
        if n % t == 0:
            return t
    return n


def _matmul(a, w, *, tm, tn, out_dtype=BF16, name="matmul"):
    m, k = a.shape
    n = w.shape[1]
    return pl.pallas_call(
        _mm_kernel,
        out_shape=jax.ShapeDtypeStruct((m, n), out_dtype),
        grid=(m // tm, n // tn),
        in_specs=[pl.BlockSpec((tm, k), lambda i, j: (i, 0)), pl.BlockSpec((k, tn), lambda i, j: (0, j))],
        out_specs=pl.BlockSpec((tm, tn), lambda i, j: (i, j)),
        compiler_params=_cparams(("parallel", "parallel")),
        name=name,
    )(a, w)


def _merge_kernel(oa_ref, wa_ref, ob_ref, wb_ref, ga_ref, gb_ref, o_ref):
    ya = _dot(oa_ref[...], wa_ref[...])
    yb = _dot(ob_ref[...], wb_ref[...])
    ga = jax.nn.sigmoid(ga_ref[...].astype(F32))
    gb = jax.nn.sigmoid(gb_ref[...].astype(F32))
    o_ref[...] = (ga * ya + gb * yb).astype(o_ref.dtype)


def _branch_merge(o_a, w_a, o_b, w_b, p, col_ga, col_gb, *, tm, tn):
    m, ka = o_a.shape
    kb = o_b.shape[1]
    n = w_a.shape[1]
    ja, jb = col_ga // tn, col_gb // tn
    return pl.pallas_call(
        _merge_kernel,
        out_shape=jax.ShapeDtypeStruct((m, n), BF16),
        grid=(m // tm, n // tn),
        in_specs=[
            pl.BlockSpec((tm, ka), lambda i, j: (i, 0)),
            pl.BlockSpec((ka, tn), lambda i, j: (0, j)),
            pl.BlockSpec((tm, kb), lambda i, j: (i, 0)),
            pl.BlockSpec((kb, tn), lambda i, j: (0, j)),
            pl.BlockSpec((tm, tn), lambda i, j: (i, ja + j)),
            pl.BlockSpec((tm, tn), lambda i, j: (i, jb + j)),
        ],
        out_specs=pl.BlockSpec((tm, tn), lambda i, j: (i, j)),
        compiler_params=_cparams(("parallel", "parallel")),
        name="branch_merge",
    )(o_a, w_a, o_b, w_b, p, p)


def _ffn_up_kernel(*refs, tn, f_expert, has_combine):
    if has_combine:
        h_ref, wg_ref, wu_ref, cmb_ref, o_ref = refs
    else:
        h_ref, wg_ref, wu_ref, o_ref = refs
    h = h_ref[...]
    act = _silu(_dot(h, wg_ref[0])) * _dot(h, wu_ref[0])
    if has_combine:
        e = (pl.program_id(1) * tn) // f_expert
        cmb = cmb_ref[...]
        lane = lax.broadcasted_iota(jnp.int32, cmb.shape, 1)
        act = act * jnp.sum(jnp.where(lane == e, cmb, 0.0), axis=-1, keepdims=True)
    o_ref[...] = act.astype(o_ref.dtype)


def _ffn_up(h, w_gate, w_up, combine, *, tm, tn):
    m, k = h.shape
    n_exp, _, f_expert = w_gate.shape
    n = n_exp * f_expert
    per = f_expert // tn
    has_combine = combine is not None
    wspec = pl.BlockSpec((1, k, tn), lambda i, j: (j // per, 0, j % per))
    in_specs = [pl.BlockSpec((tm, k), lambda i, j: (i, 0)), wspec, wspec]
    args = [h, w_gate, w_up]
    if has_combine:
        in_specs.append(pl.BlockSpec((tm, AUX_W), lambda i, j: (i, 0)))
        args.append(combine)
    return pl.pallas_call(
        functools.partial(_ffn_up_kernel, tn=tn, f_expert=f_expert, has_combine=has_combine),
        out_shape=jax.ShapeDtypeStruct((m, n), BF16),
        grid=(m // tm, n // tn),
        in_specs=in_specs,
        out_specs=pl.BlockSpec((tm, tn), lambda i, j: (i, j)),
        compiler_params=_cparams(("parallel", "parallel")),
        name="ffn_up",
    )(*args)


def _router_kernel(lg_ref, o_ref, *, n_experts):
    lg = lg_ref[...]
    lane = lax.broadcasted_iota(jnp.int32, lg.shape, 1)
    neg = jnp.float32(-jnp.inf)
    valid = lane < n_experts
    v = jnp.where(valid, lg, neg)
    m1 = jnp.max(v, axis=-1, keepdims=True)
    i1 = jnp.min(jnp.where(v == m1, lane, AUX_W), axis=-1, keepdims=True)
    v2 = jnp.where(lane == i1, neg, v)
    m2 = jnp.max(v2, axis=-1, keepdims=True)
    i2 = jnp.min(jnp.where(v2 == m2, lane, AUX_W), axis=-1, keepdims=True)
    e2 = jnp.exp(m2 - m1)
    p1 = 1.0 / (1.0 + e2)
    p2 = e2 / (1.0 + e2)
    o_ref[...] = jnp.where(lane == i1, p1, 0.0) + jnp.where(lane == i2, p2, 0.0)


def _router(logits, n_experts):
    m = logits.shape[0]
    blk = pl.BlockSpec((ROW_BLOCK, AUX_W), lambda i: (i, 0))
    return pl.pallas_call(
        functools.partial(_router_kernel, n_experts=n_experts),
        out_shape=jax.ShapeDtypeStruct((m, AUX_W), F32),
        grid=(m // ROW_BLOCK,),
        in_specs=[blk],
        out_specs=blk,
        compiler_params=_cparams(("parallel",)),
        name="router_top2",
    )(logits)


def _gates_kernel(ab_ref, par_ref, o_ref):
    x = ab_ref[...]
    rows, width = x.shape
    lane = lax.broadcasted_iota(jnp.int32, x.shape, 1)
    pos = lax.broadcasted_iota(jnp.int32, x.shape, 0) % CHUNK_A
    beta = jax.nn.sigmoid(x)
    z = x + par_ref[1:2, :]
    softplus = jnp.maximum(z, 0.0) + jnp.log1p(jnp.exp(-jnp.abs(z)))
    g = -jnp.exp(par_ref[0:1, :]) * softplus
    pre = g
    suf = g
    s = 1
    while s < CHUNK_A:
        pre = pre + jnp.where(pos >= s, pltpu.roll(pre, s, 0), 0.0)
        suf = suf + jnp.where(pos < CHUNK_A - s, pltpu.roll(suf, rows - s, 0), 0.0)
        s *= 2
    o_ref[...] = jnp.where(lane < 2 * H_A, beta, jnp.where(lane < 3 * H_A, pre, suf))


def _gates(ab, a_log, dt_bias):
    m = ab.shape[0]
    par = jnp.zeros((2, AUX_W), F32)
    par = par.at[0, 2 * H_A:4 * H_A].set(a_log.reshape(-1).astype(F32))
    par = par.at[1, 2 * H_A:4 * H_A].set(dt_bias.reshape(-1).astype(F32))
    blk = pl.BlockSpec((ROW_BLOCK, AUX_W), lambda i: (i, 0))
    return pl.pallas_call(
        _gates_kernel,
        out_shape=jax.ShapeDtypeStruct((m, AUX_W), F32),
        grid=(m // ROW_BLOCK,),
        in_specs=[blk, pl.BlockSpec((2, AUX_W), lambda i: (0, 0))],
        out_specs=blk,
        compiler_params=_cparams(("parallel",)),
        name="delta_gates",
    )(ab, par)


_HALO = 16


def _conv_kernel(prev_ref, cur_ref, next_ref, w_ref, o_ref, *, cb, nb, ncb, n_qk_blocks, n_q_blocks):
    t = pl.program_id(1)
    c = pl.program_id(2)
    prev_ok = jnp.logical_and(t != 0, t != ncb)
    next_ok = jnp.logical_and(t != ncb - 1, t != nb - 1)
    prev = jnp.where(prev_ok, prev_ref[0].astype(F32), 0.0)
    nxt = jnp.where(next_ok, next_ref[0].astype(F32), 0.0)
    x = jnp.concatenate([prev, cur_ref[0].astype(F32), nxt], axis=0)
    rows = x.shape[0]
    w = w_ref[...]
    y = None
    for d in range(CONV_W):
        shift = (CONV_W // 2 - d) % rows
        xs = x if shift == 0 else pltpu.roll(x, shift, 0)
        term = xs[_HALO:_HALO + ROW_BLOCK] * w[d:d + 1, :]
        y = term if y is None else y + term
    y = _silu(y)
    is_qk = c < n_qk_blocks
    q_scale = jnp.where(c < n_q_blocks, jnp.float32(DK_A ** -0.5), jnp.float32(1.0))
    for hh in range(cb // DK_A):
        s = y[:, hh * DK_A:(hh + 1) * DK_A]
        inv = lax.rsqrt(jnp.sum(s * s, axis=-1, keepdims=True) + EPS) * q_scale
        o_ref[0, :, hh * DK_A:(hh + 1) * DK_A] = (s * jnp.where(is_qk, inv, 1.0)).astype(o_ref.dtype)


def _conv_prep(p3, conv_w, nb, ncb, cb=512):
    b, lt, _ = p3.shape
    wc = conv_w.shape[1]
    per = ROW_BLOCK // _HALO
    n_halo = lt // _HALO
    kern = functools.partial(_conv_kernel, cb=cb, nb=nb, ncb=ncb,
                             n_qk_blocks=2 * H_A * DK_A // cb, n_q_blocks=H_A * DK_A // cb)
    return pl.pallas_call(
        kern,
        out_shape=jax.ShapeDtypeStruct((b, lt, wc), BF16),
        grid=(b, nb, wc // cb),
        in_specs=[
            pl.BlockSpec((1, _HALO, cb), lambda i, t, c: (i, jnp.maximum(t * per - 1, 0), c)),
            pl.BlockSpec((1, ROW_BLOCK, cb), lambda i, t, c: (i, t, c)),
            pl.BlockSpec((1, _HALO, cb), lambda i, t, c: (i, jnp.minimum((t + 1) * per, n_halo - 1), c)),
            pl.BlockSpec((CONV_W, cb), lambda i, t, c: (0, c)),
        ],
        out_specs=pl.BlockSpec((1, ROW_BLOCK, cb), lambda i, t, c: (i, t, c)),
        compiler_params=_cparams(("parallel", "parallel", "parallel")),
        name="conv_prep",
    )(p3, p3, p3, conv_w)


def _delta_prepare(items, wq_ref, u_ref, at_ref, kd_ref, ge_ref):
    c = CHUNK_A
    row = lax.broadcasted_iota(jnp.int32, (c, 2 * c), 0)
    col = lax.broadcasted_iota(jnp.int32, (c, 2 * c), 1) % c
    eye = (row == col).astype(F32)
    decay, t_inv, pw = [], [], []
    for (_, reverse, q, k, v, beta, gc, gc_row) in items:
        incl = (row <= col) if reverse else (row >= col)
        decay.append(jnp.exp(jnp.where(incl, gc - gc_row, -1e30)))
    k2 = [jnp.concatenate([it[3], it[3]], axis=0) for it in items]
    kk = [_dot_nt(it[3], kd) for it, kd in zip(items, k2)]
    qk = [_dot_nt(it[2], kd) for it, kd in zip(items, k2)]
    for i, (_, reverse, q, k, v, beta, gc, gc_row) in enumerate(items):
        strict = (row < col) if reverse else (row > col)
        n = jnp.where(strict, beta * kk[i] * decay[i], 0.0)
        pw.append(n)
        t_inv.append(eye - n)
    pw = [_dot3(p, p) for p in pw]
    span = 4
    while span < c:
        prod = [_dot3(jnp.concatenate([p, t], axis=0), p) for p, t in zip(pw, t_inv)]
        pw = [r[0:c] for r in prod]
        t_inv = [t + r[c:2 * c] for t, r in zip(t_inv, prod)]
        span *= 2
    t_inv = [t + _dot3(t, p) for t, p in zip(t_inv, pw)]
    rhs, e_gc = [], []
    for (_, reverse, q, k, v, beta, gc, gc_row) in items:
        e = jnp.exp(gc)
        e_gc.append(e)
        r = jnp.concatenate([v.astype(F32) * beta, k.astype(F32) * (beta * e)], axis=-1).astype(BF16)
        rhs.append(jnp.concatenate([r, r], axis=0))
    uw = [_dot(_hilo_lanes(t), r) for t, r in zip(t_inv, rhs)]
    for i, (slot, reverse, q, k, v, beta, gc, gc_row) in enumerate(items):
        last = 0 if reverse else c - 1
        g_last = gc[last:last + 1, :]
        u_ref[slot] = uw[i][:, :DV_A]
        wq_ref[slot, 0:c, :] = uw[i][:, DV_A:].astype(BF16)
        wq_ref[slot, c:2 * c, :] = (q.astype(F32) * e_gc[i]).astype(BF16)
        at_ref[slot] = (qk[i] * decay[i])[:, 0:c].astype(BF16)
        kd_ref[slot] = (k.astype(F32) * jnp.exp(g_last - gc)).astype(BF16)
        ge_ref[slot] = jnp.broadcast_to(jnp.exp(g_last), ge_ref.shape[1:])


def _delta_kernel(qf_ref, kf_ref, vf_ref, gcf_ref, grf_ref, qb_ref, kb_ref, vb_ref, gcb_ref, grb_ref,
                  of_ref, ob_ref, s_ref, wq_ref, u_ref, at_ref, kd_ref, ge_ref, *, heads):
    @pl.when(pl.program_id(2) == 0)
    def _():
        s_ref[...] = jnp.zeros_like(s_ref)

    c = CHUNK_A
    nchunks = ROW_BLOCK // c
    fwd = (qf_ref, kf_ref, vf_ref, gcf_ref, grf_ref, of_ref)
    bwd = (qb_ref, kb_ref, vb_ref, gcb_ref, grb_ref, ob_ref)

    def slot(hh, reverse, cc):
        return (hh * 2 + int(reverse)) * nchunks + cc

    for h0 in range(0, heads, DELTA_WAVE):
        items = []
        for hh in range(h0, min(h0 + DELTA_WAVE, heads)):
            for reverse in (False, True):
                q_ref, k_ref, v_ref, gc_ref, gr_ref, _ = bwd if reverse else fwd
                bcol, gcol = (1, 3) if reverse else (0, 2)
                hs = slice(hh * DK_A, (hh + 1) * DK_A)
                for cc in range(nchunks):
                    rows = slice(cc * c, (cc + 1) * c)
                    gcols = gc_ref[0, hh, rows, :]
                    items.append((slot(hh, reverse, cc), reverse, q_ref[0, rows, hs], k_ref[0, rows, hs],
                                  v_ref[0, rows, hs], gcols[:, bcol:bcol + 1], gcols[:, gcol:gcol + 1],
                                  gr_ref[0, hh, cc, gcol:gcol + 1, :]))
        _delta_prepare(items, wq_ref, u_ref, at_ref, kd_ref, ge_ref)

    chains = [(hh, reverse) for hh in range(heads) for reverse in (False, True)]
    states = [s_ref[hh * 2 + int(reverse)] for hh, reverse in chains]
    for ci in range(nchunks):
        slots = [slot(hh, reverse, nchunks - 1 - ci if reverse else ci) for hh, reverse in chains]
        sbs = [s.astype(BF16) for s in states]
        ws = [_dot(wq_ref[sl], sb) for sl, sb in zip(slots, sbs)]
        vns = [(u_ref[sl] - r[0:c]).astype(BF16) for sl, r in zip(slots, ws)]
        intra = [_dot(at_ref[sl], vn) for sl, vn in zip(slots, vns)]
        upd = [_dot_tn(kd_ref[sl], vn) for sl, vn in zip(slots, vns)]
        for j, (hh, reverse) in enumerate(chains):
            cc = nchunks - 1 - ci if reverse else ci
            o_ref = bwd[5] if reverse else fwd[5]
            o_ref[0, cc * c:(cc + 1) * c, hh * DV_A:(hh + 1) * DV_A] = (ws[j][c:2 * c] + intra[j]).astype(o_ref.dtype)
            states[j] = states[j] * ge_ref[slots[j], 0:1, :] + upd[j]
    for j, (hh, reverse) in enumerate(chains):
        s_ref[hh * 2 + int(reverse)] = states[j]


def _bwd_block(t, nb, ncb):
    return jnp.where(t < ncb, ncb - 1 - t, nb - 1 - (t - ncb))


def _delta_scan(qkv, g_col, g_row, nb, ncb):
    b, lt, _ = qkv.shape
    nch = ROW_BLOCK // CHUNK_A
    hg = _pick(H_A, (DELTA_HEADS, 2, 1))
    ngrp = H_A // hg
    n_slots = hg * 2 * nch

    def blk(rev, t):
        return _bwd_block(t, nb, ncb) if rev else t

    def tok(grp0, rev):
        return pl.BlockSpec((1, ROW_BLOCK, hg * DK_A), lambda i, h, t: (i, blk(rev, t), grp0 + h))

    def gcol(rev):
        return pl.BlockSpec((1, hg, ROW_BLOCK, 4), lambda i, h, t: (i, h, blk(rev, t), 0))

    def grow(rev):
        return pl.BlockSpec((1, hg, nch, 4, 2 * CHUNK_A), lambda i, h, t: (i, h, blk(rev, t), 0, 0))

    in_specs, args = [], []
    for rev in (False, True):
        in_specs += [tok(0, rev), tok(ngrp, rev), tok(2 * ngrp, rev), gcol(rev), grow(rev)]
        args += [qkv, qkv, qkv, g_col, g_row]
    o_shape = jax.ShapeDtypeStruct((b, lt, H_A * DV_A), BF16)
    return pl.pallas_call(
        functools.partial(_delta_kernel, heads=hg),
        out_shape=[o_shape, o_shape],
        grid=(b, ngrp, nb),
        in_specs=in_specs,
        out_specs=[tok(0, False), tok(0, True)],
        scratch_shapes=[
            pltpu.VMEM((hg * 2, DK_A, DV_A), F32),
            pltpu.VMEM((n_slots, 2 * CHUNK_A, DK_A), BF16),
            pltpu.VMEM((n_slots, CHUNK_A, DV_A), F32),
            pltpu.VMEM((n_slots, CHUNK_A, CHUNK_A), BF16),
            pltpu.VMEM((n_slots, CHUNK_A, DK_A), BF16),
            pltpu.VMEM((n_slots, 8, LANES), F32),
        ],
        compiler_params=_cparams(("parallel", "parallel", "arbitrary")),
        name="delta_scan",
    )(*args)


def _delta_out_kernel(of_ref, ob_ref, z_ref, w_ref, o_ref, *, cb):
    o = of_ref[...].astype(F32) + ob_ref[...].astype(F32)
    z = z_ref[...].astype(F32)
    w = w_ref[...]
    for hh in range(cb // DV_A):
        sl = slice(hh * DV_A, (hh + 1) * DV_A)
        s = o[:, sl]
        y = s * lax.rsqrt(jnp.mean(s * s, axis=-1, keepdims=True) + EPS) * w
        o_ref[:, sl] = (y * _silu(z[:, sl])).astype(o_ref.dtype)


def _delta_out(o_f, o_b, p, col_z, dn_norm_w, cb=512):
    m, wa = o_f.shape
    blk = pl.BlockSpec((ROW_BLOCK, cb), lambda i, c: (i, c))
    j0 = col_z // cb
    return pl.pallas_call(
        functools.partial(_delta_out_kernel, cb=cb),
        out_shape=jax.ShapeDtypeStruct((m, wa), BF16),
        grid=(m // ROW_BLOCK, wa // cb),
        in_specs=[blk, blk, pl.BlockSpec((ROW_BLOCK, cb), lambda i, c: (i, j0 + c)),
                  pl.BlockSpec((1, DV_A), lambda i, c: (0, 0))],
        out_specs=blk,
        compiler_params=_cparams(("parallel", "parallel")),
        name="delta_out",
    )(o_f, o_b, p, dn_norm_w.reshape(1, DV_A).astype(F32))


def _rotate(t, cos, sin):
    half = t.shape[-1] // 2
    t1, t2 = t[:, :half], t[:, half:]
    return jnp.concatenate([t1 * cos - t2 * sin, t2 * cos + t1 * sin], axis=-1)


def _ret_kernel(qf_ref, kf_ref, vf_ref, cf_ref, sf_ref, qb_ref, kb_ref, vb_ref, cb_ref, sb_ref, lg_ref,
                of_ref, ob_ref, stf_ref, stb_ref):
    @pl.when(pl.program_id(2) == 0)
    def _():
        stf_ref[...] = jnp.zeros_like(stf_ref)
        stb_ref[...] = jnp.zeros_like(stb_ref)

    c = ROW_BLOCK
    k_scale = DK_B ** -0.5
    row = lax.broadcasted_iota(jnp.int32, (c, c), 0)
    col = lax.broadcasted_iota(jnp.int32, (c, c), 1)
    pos0 = lax.broadcasted_iota(jnp.int32, (c, 1), 0)
    dirs = ((False, qf_ref, kf_ref, vf_ref, cf_ref, sf_ref, of_ref, stf_ref),
            (True, qb_ref, kb_ref, vb_ref, cb_ref, sb_ref, ob_ref, stb_ref))
    qs, ks, lgs, posfs, decays = [], [], [], [], []
    for reverse, q_ref, k_ref, v_ref, c_ref, s_ref, o_ref, st_ref in dirs:
        cos, sin = c_ref[...], s_ref[...]
        qs.append(_rotate(q_ref[0].astype(F32), cos, sin))
        ks.append(_rotate(k_ref[0].astype(F32) * k_scale, cos, sin))
        lg = lg_ref[0, 1 if reverse else 0, 0:1, 0:1]
        lgs.append(lg)
        dist = (col - row) if reverse else (row - col)
        decays.append(jnp.exp(jnp.where(dist >= 0, dist.astype(F32) * lg, -1e30)))
        posfs.append(((c - 1 - pos0) if reverse else pos0).astype(F32))
    vs = [d[3][0] for d in dirs]
    states = [d[7][...] for d in dirs]
    qk = [_dot_nt(q.astype(BF16), k.astype(BF16)) for q, k in zip(qs, ks)]
    inter = [_dot((q * jnp.exp((p + 1.0) * lg)).astype(BF16), s.astype(BF16))
             for q, p, lg, s in zip(qs, posfs, lgs, states)]
    upd = [_dot_tn((k * jnp.exp((c - 1.0 - p) * lg)).astype(BF16), v)
           for k, p, lg, v in zip(ks, posfs, lgs, vs)]
    intra = [_dot((a * dec).astype(BF16), v) for a, dec, v in zip(qk, decays, vs)]
    for i, d in enumerate(dirs):
        d[6][0] = (intra[i] + inter[i]).astype(d[6].dtype)
        d[7][...] = states[i] * jnp.exp(c * lgs[i]) + upd[i]


def _ret_scan(p3, col_q, col_k, col_v, cos, sin, lg, nb, ncb):
    b, lt, _ = p3.shape
    jq, jk, jv = col_q // DK_B, col_k // DK_B, col_v // DV_B

    def tok(j0, rev):
        if rev:
            return pl.BlockSpec((1, ROW_BLOCK, DK_B), lambda i, h, t: (i, _bwd_block(t, nb, ncb), j0 + h))
        return pl.BlockSpec((1, ROW_BLOCK, DK_B), lambda i, h, t: (i, t, j0 + h))

    def ang(rev):
        if rev:
            return pl.BlockSpec((ROW_BLOCK, DK_B // 2), lambda i, h, t: (_bwd_block(t, nb, ncb), 0))
        return pl.BlockSpec((ROW_BLOCK, DK_B // 2), lambda i, h, t: (t, 0))

    in_specs, args = [], []
    for rev in (False, True):
        in_specs += [tok(jq, rev), tok(jk, rev), tok(jv, rev), ang(rev), ang(rev)]
        args += [p3, p3, p3, cos, sin]
    in_specs.append(pl.BlockSpec((1, 2, 8, LANES), lambda i, h, t: (h, 0, 0, 0)))
    args.append(lg)
    o_shape = jax.ShapeDtypeStruct((b, lt, H_B * DV_B), BF16)
    return pl.pallas_call(
        _ret_kernel,
        out_shape=[o_shape, o_shape],
        grid=(b, H_B, nb),
        in_specs=in_specs,
        out_specs=[tok(0, False), tok(0, True)],
        scratch_shapes=[pltpu.VMEM((DK_B, DV_B), F32), pltpu.VMEM((DK_B, DV_B), F32)],
        compiler_params=_cparams(("parallel", "parallel", "arbitrary")),
        name="retention_scan",
    )(*args)


def _ret_out_kernel(of_ref, ob_ref, g_ref, w_ref, b_ref, o_ref, *, cb):
    o = of_ref[...].astype(F32) + ob_ref[...].astype(F32)
    g = g_ref[...].astype(F32)
    for hh in range(cb // DV_B):
        sl = slice(hh * DV_B, (hh + 1) * DV_B)
        s = o[:, sl]
        d = s - jnp.mean(s, axis=-1, keepdims=True)
        y = d * lax.rsqrt(jnp.mean(d * d, axis=-1, keepdims=True) + GN_EPS)
        o_ref[:, sl] = ((y * w_ref[:, sl] + b_ref[:, sl]) * _silu(g[:, sl])).astype(o_ref.dtype)


def _ret_out(o_f, o_b, p, col_g, gn_w, gn_b, cb=512):
    m, wb = o_f.shape
    blk = pl.BlockSpec((ROW_BLOCK, cb), lambda i, c: (i, c))
    par = pl.BlockSpec((1, cb), lambda i, c: (0, c))
    j0 = col_g // cb
    return pl.pallas_call(
        functools.partial(_ret_out_kernel, cb=cb),
        out_shape=jax.ShapeDtypeStruct((m, wb), BF16),
        grid=(m // ROW_BLOCK, wb // cb),
        in_specs=[blk, blk, pl.BlockSpec((ROW_BLOCK, cb), lambda i, c: (i, j0 + c)), par, par],
        out_specs=blk,
        compiler_params=_cparams(("parallel", "parallel")),
        name="retention_out",
    )(o_f, o_b, p, gn_w.reshape(1, wb).astype(F32), gn_b.reshape(1, wb).astype(F32))


def _rope_tables(ctx_len, seq):
    n_freq = DK_B // 4
    freqs = ROPE_BASE ** (-jnp.arange(n_freq, dtype=F32) / n_freq)
    rows = seq // GRID_W
    r = jnp.repeat(jnp.arange(rows, dtype=F32), GRID_W)
    col = jnp.tile(jnp.arange(GRID_W, dtype=F32), rows)
    ang = jnp.concatenate([r[:, None] * freqs, col[:, None] * freqs], axis=-1)
    ang = jnp.concatenate([jnp.zeros((ctx_len, DK_B // 2), F32), ang], axis=0)
    return jnp.cos(ang), jnp.sin(ang)


def _retention_log_decays():
    h = jnp.arange(H_B, dtype=F32)
    fwd = jnp.log1p(-jnp.exp2(-5.0 - h))
    bwd = jnp.log1p(-jnp.exp2(-5.5 - h))
    lg = jnp.stack([fwd, bwd], axis=1)
    return jnp.broadcast_to(lg[:, :, None, None], (H_B, 2, 8, LANES))


def kernel(x, c, ctx, c_ctx, mod_down, mod_up, mod_bias, norm_gains, w_in, conv_w, a_log, dt_bias, dn_norm_w,
           gn_w, gn_b, w_br_a, w_br_b, w_out, ffn_gate, ffn_up, ffn_down, router, moe_gate, moe_up, moe_down):
    b, seq, d = x.shape
    ctx_len = ctx.shape[1]
    depth = w_in.shape[0]
    lt = ctx_len + seq
    assert ctx_len % ROW_BLOCK == 0 and seq % ROW_BLOCK == 0 and seq % GRID_W == 0 and b + 1 <= 8
    nb, ncb = lt // ROW_BLOCK, ctx_len // ROW_BLOCK
    m = b * lt
    w_conv = 2 * H_A * DK_A + H_A * DV_A
    w_a, w_b = H_A * DV_A, H_B * DV_B
    n_ab = 4 * H_A
    col_z = w_conv
    n_p1 = col_z + w_a
    col_kb = H_B * DK_B
    col_vb = col_kb + H_B * DK_B
    col_gb = col_vb + w_b
    col_ga = col_gb + w_b
    col_gate_b = col_ga + d
    n_p2 = col_gate_b + d

    tm = _pick(m, (768, 512, 256))

    cond = jnp.zeros((8, d), F32).at[:b].set(c).at[b].set(c_ctx)
    mod = _modulation(cond, mod_down, mod_up, mod_bias)[:, :b + 1].reshape(depth, b + 1, 6, d)

    cos, sin = _rope_tables(ctx_len, seq)
    lg = _retention_log_decays()

    w_br_a, w_br_b, w_out = _to_bf16(w_br_a), _to_bf16(w_br_b), _to_bf16(w_out)
    ffn_gate, ffn_up, ffn_down = _to_bf16(ffn_gate), _to_bf16(ffn_up), _to_bf16(ffn_down)
    moe_gate, moe_up, moe_down = _to_bf16(moe_gate), _to_bf16(moe_up), _to_bf16(moe_down)

    def in_weights(i):
        w = w_in[i]
        w_ab = jnp.zeros((d, AUX_W), BF16).at[:, :n_ab].set(w[:, n_p1:n_p1 + n_ab].astype(BF16))
        return w[:, :n_p1].astype(BF16), w[:, n_p1 + n_ab:].astype(BF16), w_ab

    def mix_params(i, idx):
        return jnp.stack([mod[i, :, idx[0]], mod[i, :, idx[1]], mod[i, :, idx[2]]], axis=1)

    w_p1, w_p2, w_ab = in_weights(0)
    mp0 = jnp.stack([jnp.zeros_like(mod[0, :, 0]), mod[0, :, 0], mod[0, :, 1]], axis=1)
    xs, h, ab = _row_stage(x.reshape(b * seq, d), None, mp0, jnp.stack([norm_gains[0, 0], norm_gains[0, 0]]),
                           w_ab, nb, ncb, has_h=True, ctx_rows=ctx.reshape(b * ctx_len, d))

    for i in range(depth):
        gains = norm_gains[i]
        tn_in = (1024, 512, 256, 128)
        p = _matmul(h, w_p1, tm=tm, tn=_pick(n_p1, tn_in), name="in_proj_a")
        p2 = _matmul(h, w_p2, tm=tm, tn=_pick(n_p2, tn_in), name="in_proj_b")
        p3 = p.reshape(b, lt, n_p1)
        p23 = p2.reshape(b, lt, n_p2)
        g = _gates(ab, a_log[i], dt_bias[i])[:, :n_ab].reshape(b, lt, 4, H_A)
        g_col = g.transpose(0, 3, 1, 2)
        g_row = g.reshape(b, lt // CHUNK_A, CHUNK_A, 4, H_A).transpose(0, 4, 1, 3, 2)
        g_row = jnp.concatenate([g_row, g_row], axis=-1)
        qkv = _conv_prep(p3, conv_w[i], nb, ncb, cb=_pick(H_A * DK_A, (512, 256, 128)))
        of, ob = _delta_scan(qkv, g_col, g_row, nb, ncb)
        o_a = _delta_out(of.reshape(m, w_a), ob.reshape(m, w_a), p, col_z, dn_norm_w[i],
                         cb=_pick(w_a, (512, 256, 128)))
        rf, rb = _ret_scan(p23, 0, col_kb, col_vb, cos, sin, lg, nb, ncb)
        o_b = _ret_out(rf.reshape(m, w_b), rb.reshape(m, w_b), p2, col_gb, gn_w[i], gn_b[i],
                       cb=_pick(w_b, (512, 256)))
        tn_d = _pick(d, (1024, 512, 256, 128))
        merged = _branch_merge(o_a, w_br_a[i], o_b, w_br_b[i], p2, col_ga, col_gate_b,
                               tm=tm, tn=_pick(d, (512, 256, 128)))
        y = _matmul(merged, w_out[i], tm=tm, tn=tn_d, name="out_proj")
        j = i // 2
        is_moe = i % 2 == 1
        gn = jnp.stack([gains[1], gains[2]])
        if is_moe:
            n_exp = router.shape[2]
            w_r = jnp.zeros((d, AUX_W), BF16).at[:, :n_exp].set(router[j].astype(BF16))
            xs, h2, logits = _row_stage(xs, y, mix_params(i, (2, 3, 4)), gn, w_r, nb, ncb, has_h=True)
            combine = _router(logits, n_exp)
            wg, wu = moe_gate[j], moe_up[j]
            wd = moe_down[j].reshape(-1, d)
        else:
            xs, h2 = _row_stage(xs, y, mix_params(i, (2, 3, 4)), gn, None, nb, ncb, has_h=True)
            combine = None
            wg, wu, wd = ffn_gate[j][None], ffn_up[j][None], ffn_down[j]
        act = _ffn_up(h2, wg, wu, combine, tm=tm, tn=_pick(wg.shape[2], (512, 256, 128)))
        y2 = _matmul(act, wd, tm=tm, tn=_pick(d, (512, 256, 128)), name="ffn_down")
        if i + 1 < depth:
            w_p1, w_p2, w_ab = in_weights(i + 1)
            mp = jnp.stack([mod[i, :, 5], mod[i + 1, :, 0], mod[i + 1, :, 1]], axis=1)
            gn = jnp.stack([gains[3], norm_gains[i + 1, 0]])
            xs, h, ab = _row_stage(xs, y2, mp, gn, w_ab, nb, ncb, has_h=True)
        else:
            mp = jnp.stack([mod[i, :, 5], mod[i, :, 5], mod[i, :, 5]], axis=1)
            (out,) = _row_stage(xs, y2, mp, jnp.stack([gains[3], gains[3]]), None, nb, ncb, has_h=False,
                                latent_only=True)

    return out.reshape(b, seq, d)
```

```python
import functools

import jax
import jax.numpy as jnp
from jax import lax
from jax.experimental import pallas as pl
from jax.experimental.pallas import tpu as pltpu

GRID_W = 64
CONV_W = 5
H_A = 16
DK_A = 128
DV_A = 128
H_B = 8
DK_B = 256
DV_B = 256
TOP_K = 2
ROPE_BASE = 10000.0
EPS = 1e-6
GN_EPS = 1e-5

LANES = 128
ROW_BLOCK = 256
CHUNK_A = 64
DELTA_HEADS = 4
DELTA_WAVE = 2
AUX_W = LANES
VMEM_LIMIT = 56 * 1024 * 1024

F32 = jnp.float32
BF16 = jnp.bfloat16


def _cparams(sem):
    return pltpu.CompilerParams(dimension_semantics=sem, vmem_limit_bytes=VMEM_LIMIT)


def _silu(t):
    return t * jax.nn.sigmoid(t)


def _dot(a, b):
    return jnp.dot(a, b, preferred_element_type=F32)


def _dot_nt(a, b):
    return lax.dot_general(a, b, (((1,), (1,)), ((), ())), preferred_element_type=F32)


def _dot_tn(a, b):
    return lax.dot_general(a, b, (((0,), (0,)), ((), ())), preferred_element_type=F32)


def _hilo_lanes(a):
    hif = a.astype(BF16).astype(F32)
    lane = lax.broadcasted_iota(jnp.int32, a.shape, 1)
    return jnp.where(lane < a.shape[1] // 2, hif, a - hif).astype(BF16)


def _dot3(a, b):
    bh = b.astype(BF16)
    bl = (b - bh.astype(F32)).astype(BF16)
    lhs = jnp.concatenate([_hilo_lanes(a), a.astype(BF16)], axis=1)
    rhs = jnp.concatenate([bh, bh, bl, jnp.zeros_like(bh)], axis=0)
    return _dot(lhs, rhs)


def _mod_kernel(cond_ref, wd_ref, wu_ref, b_ref, o_ref, t_ref):
    @pl.when(pl.program_id(1) == 0)
    def _():
        s = _silu(cond_ref[...])
        t_ref[...] = _dot(s.astype(BF16), wd_ref[0].astype(BF16))

    o_ref[0] = _dot(t_ref[...].astype(BF16), wu_ref[0].astype(BF16)) + b_ref[0]


def _modulation(cond, w_down, w_up, bias):
    depth, d, r = w_down.shape
    n = w_up.shape[2]
    tn = _pick(n, (2048, 1024, 512, 256, 128))
    return pl.pallas_call(
        _mod_kernel,
        out_shape=jax.ShapeDtypeStruct((depth, 8, n), F32),
        grid=(depth, n // tn),
        in_specs=[
            pl.BlockSpec((8, d), lambda l, j: (0, 0)),
            pl.BlockSpec((1, d, r), lambda l, j: (l, 0, 0)),
            pl.BlockSpec((1, r, tn), lambda l, j: (l, 0, j)),
            pl.BlockSpec((1, 1, tn), lambda l, j: (l, 0, j)),
        ],
        out_specs=pl.BlockSpec((1, 8, tn), lambda l, j: (l, 0, j)),
        scratch_shapes=[pltpu.VMEM((8, r), F32)],
        compiler_params=_cparams(("arbitrary", "arbitrary")),
        name="modulation",
    )(cond, w_down, w_up, bias.reshape(depth, 1, n))


def _rms(t, gain):
    return t * lax.rsqrt(jnp.mean(t * t, axis=-1, keepdims=True) + EPS) * gain


def _row_kernel(*refs, has_y, has_h, has_aux, nb, ncb, split_in):
    it = iter(refs)
    c_ref = next(it) if split_in else None
    x_ref = next(it)
    y_ref = next(it) if has_y else None
    mp_ref = next(it)
    gn_ref = next(it)
    waux_ref = next(it) if has_aux else None
    xo_ref = next(it) if (has_y or split_in) else None
    h_ref = next(it) if has_h else None
    aux_ref = next(it) if has_aux else None

    x = x_ref[...]
    if split_in:
        x = jnp.where(pl.program_id(0) % nb < ncb, c_ref[...], x)
    if has_y:
        x = x + mp_ref[0, 0:1, :] * _rms(y_ref[...].astype(F32), gn_ref[0:1, :])
    if xo_ref is not None:
        xo_ref[...] = x
    if has_h:
        h = _rms(x, gn_ref[1:2, :]) * (1.0 + mp_ref[0, 2:3, :]) + mp_ref[0, 1:2, :]
        hb = h.astype(BF16)
        h_ref[...] = hb
        if has_aux:
            aux_ref[...] = _dot(hb, waux_ref[...])


def _row_stage(x, y, mp, gn, waux, nb, ncb, *, has_h, ctx_rows=None, latent_only=False):
    d = x.shape[1]
    has_y = y is not None
    has_aux = waux is not None
    split_in = ctx_rows is not None
    n_ctx_rows = mp.shape[0] - 1
    nlb = nb - ncb
    m = x.shape[0] + ctx_rows.shape[0] if split_in else x.shape[0]
    steps_per_batch = nlb if latent_only else nb
    first = ncb if latent_only else 0

    def full(i):
        return (i // steps_per_batch) * nb + first + i % steps_per_batch

    def sel(i):
        return jnp.where(full(i) % nb < ncb, n_ctx_rows, full(i) // nb)

    row = pl.BlockSpec((ROW_BLOCK, d), lambda i: (full(i), 0))
    out_row = pl.BlockSpec((ROW_BLOCK, d), lambda i: (i, 0))
    in_specs, args = [], []
    if split_in:
        in_specs += [pl.BlockSpec((ROW_BLOCK, d), lambda i: ((i // nb) * ncb + jnp.minimum(i % nb, ncb - 1), 0)),
                     pl.BlockSpec((ROW_BLOCK, d), lambda i: ((i // nb) * nlb + jnp.maximum(i % nb - ncb, 0), 0))]
        args += [ctx_rows, x]
    else:
        in_specs.append(row)
        args.append(x)
    if has_y:
        in_specs.append(row)
        args.append(y)
    in_specs += [pl.BlockSpec((1, 3, d), lambda i: (sel(i), 0, 0)), pl.BlockSpec((2, d), lambda i: (0, 0))]
    args += [mp, gn]
    if has_aux:
        in_specs.append(pl.BlockSpec((d, AUX_W), lambda i: (0, 0)))
        args.append(waux)
    n_out_rows = (m // nb) * nlb if latent_only else m
    out_shape, out_specs = [], []
    if has_y or split_in:
        out_shape.append(jax.ShapeDtypeStruct((n_out_rows, d), F32))
        out_specs.append(out_row)
    if has_h:
        out_shape.append(jax.ShapeDtypeStruct((n_out_rows, d), BF16))
        out_specs.append(out_row)
    if has_aux:
        out_shape.append(jax.ShapeDtypeStruct((n_out_rows, AUX_W), F32))
        out_specs.append(pl.BlockSpec((ROW_BLOCK, AUX_W), lambda i: (i, 0)))
    outs = pl.pallas_call(
        functools.partial(_row_kernel, has_y=has_y, has_h=has_h, has_aux=has_aux, nb=nb, ncb=ncb,
                          split_in=split_in),
        out_shape=out_shape,
        grid=(n_out_rows // ROW_BLOCK,),
        in_specs=in_specs,
        out_specs=out_specs,
        compiler_params=_cparams(("parallel",)),
        name="row_stage",
    )(*args)
    return list(outs)


def _cast_kernel(w_ref, o_ref):
    o_ref[...] = w_ref[...].astype(o_ref.dtype)


def _to_bf16(w):
    w2 = w.reshape(-1, w.shape[-1])
    r, c = w2.shape
    tr = _pick(r, (512, 256, 128, 64, 32, 16))
    tc = _pick(c, (2048, 1024, 512, 256, 128))
    blk = pl.BlockSpec((tr, tc), lambda i, j: (i, j))
    out = pl.pallas_call(
        _cast_kernel,
        out_shape=jax.ShapeDtypeStruct((r, c), BF16),
        grid=(r // tr, c // tc),
        in_specs=[blk],
        out_specs=blk,
        compiler_params=_cparams(("parallel", "parallel")),
        name="weights_to_bf16",
    )(w2)
    return out.reshape(w.shape)


def _mm_kernel(a_ref, w_ref, o_ref):
    o_ref[...] = _dot(a_ref[...], w_ref[0]).astype(o_ref.dtype)


def _pick(n, pref):
    for t in pref:
        if n % t == 0:
            return t
    return n


def _matmul(a, w, layer, *, tm, tn, out_dtype=BF16, name="matmul"):
    m, k = a.shape
    n = w.shape[2]
    return pl.pallas_call(
        _mm_kernel,
        out_shape=jax.ShapeDtypeStruct((m, n), out_dtype),
        grid=(m // tm, n // tn),
        in_specs=[pl.BlockSpec((tm, k), lambda i, j: (i, 0)), pl.BlockSpec((1, k, tn), lambda i, j: (layer, 0, j))],
        out_specs=pl.BlockSpec((tm, tn), lambda i, j: (i, j)),
        compiler_params=_cparams(("parallel", "parallel")),
        name=name,
    )(a, w)


def _merge_kernel(oa_ref, wa_ref, ob_ref, wb_ref, ga_ref, gb_ref, o_ref):
    ya = _dot(oa_ref[...], wa_ref[0])
    yb = _dot(ob_ref[...], wb_ref[0])
    ga = jax.nn.sigmoid(ga_ref[...].astype(F32))
    gb = jax.nn.sigmoid(gb_ref[...].astype(F32))
    o_ref[...] = (ga * ya + gb * yb).astype(o_ref.dtype)


def _branch_merge(o_a, w_a, o_b, w_b, layer, p, col_ga, col_gb, *, tm, tn):
    m, ka = o_a.shape
    kb = o_b.shape[1]
    n = w_a.shape[2]
    ja, jb = col_ga // tn, col_gb // tn
    return pl.pallas_call(
        _merge_kernel,
        out_shape=jax.ShapeDtypeStruct((m, n), BF16),
        grid=(m // tm, n // tn),
        in_specs=[
            pl.BlockSpec((tm, ka), lambda i, j: (i, 0)),
            pl.BlockSpec((1, ka, tn), lambda i, j: (layer, 0, j)),
            pl.BlockSpec((tm, kb), lambda i, j: (i, 0)),
            pl.BlockSpec((1, kb, tn), lambda i, j: (layer, 0, j)),
            pl.BlockSpec((tm, tn), lambda i, j: (i, ja + j)),
            pl.BlockSpec((tm, tn), lambda i, j: (i, jb + j)),
        ],
        out_specs=pl.BlockSpec((tm, tn), lambda i, j: (i, j)),
        compiler_params=_cparams(("parallel", "parallel")),
        name="branch_merge",
    )(o_a, w_a, o_b, w_b, p, p)


def _ffn_up_kernel(*refs, tn, f_expert, has_combine):
    if has_combine:
        h_ref, wg_ref, wu_ref, cmb_ref, o_ref = refs
    else:
        h_ref, wg_ref, wu_ref, o_ref = refs
    h = h_ref[...]
    act = _silu(_dot(h, wg_ref[0])) * _dot(h, wu_ref[0])
    if has_combine:
        e = (pl.program_id(1) * tn) // f_expert
        cmb = cmb_ref[...]
        lane = lax.broadcasted_iota(jnp.int32, cmb.shape, 1)
        act = act * jnp.sum(jnp.where(lane == e, cmb, 0.0), axis=-1, keepdims=True)
    o_ref[...] = act.astype(o_ref.dtype)


def _ffn_up(h, w_gate, w_up, layer, n_exp, combine, *, tm, tn):
    m, k = h.shape
    f_expert = w_gate.shape[2]
    n = n_exp * f_expert
    per = f_expert // tn
    has_combine = combine is not None
    wspec = pl.BlockSpec((1, k, tn), lambda i, j: (layer * n_exp + j // per, 0, j % per))
    in_specs = [pl.BlockSpec((tm, k), lambda i, j: (i, 0)), wspec, wspec]
    args = [h, w_gate, w_up]
    if has_combine:
        in_specs.append(pl.BlockSpec((tm, AUX_W), lambda i, j: (i, 0)))
        args.append(combine)
    return pl.pallas_call(
        functools.partial(_ffn_up_kernel, tn=tn, f_expert=f_expert, has_combine=has_combine),
        out_shape=jax.ShapeDtypeStruct((m, n), BF16),
        grid=(m // tm, n // tn),
        in_specs=in_specs,
        out_specs=pl.BlockSpec((tm, tn), lambda i, j: (i, j)),
        compiler_params=_cparams(("parallel", "parallel")),
        name="ffn_up",
    )(*args)


def _router_kernel(lg_ref, o_ref, *, n_experts):
    lg = lg_ref[...]
    lane = lax.broadcasted_iota(jnp.int32, lg.shape, 1)
    neg = jnp.float32(-jnp.inf)
    valid = lane < n_experts
    v = jnp.where(valid, lg, neg)
    m1 = jnp.max(v, axis=-1, keepdims=True)
    i1 = jnp.min(jnp.where(v == m1, lane, AUX_W), axis=-1, keepdims=True)
    v2 = jnp.where(lane == i1, neg, v)
    m2 = jnp.max(v2, axis=-1, keepdims=True)
    i2 = jnp.min(jnp.where(v2 == m2, lane, AUX_W), axis=-1, keepdims=True)
    e2 = jnp.exp(m2 - m1)
    p1 = 1.0 / (1.0 + e2)
    p2 = e2 / (1.0 + e2)
    o_ref[...] = jnp.where(lane == i1, p1, 0.0) + jnp.where(lane == i2, p2, 0.0)


def _router(logits, n_experts):
    m = logits.shape[0]
    blk = pl.BlockSpec((ROW_BLOCK, AUX_W), lambda i: (i, 0))
    return pl.pallas_call(
        functools.partial(_router_kernel, n_experts=n_experts),
        out_shape=jax.ShapeDtypeStruct((m, AUX_W), F32),
        grid=(m // ROW_BLOCK,),
        in_specs=[blk],
        out_specs=blk,
        compiler_params=_cparams(("parallel",)),
        name="router_top2",
    )(logits)


def _gates_kernel(ab_ref, par_ref, o_ref):
    x = ab_ref[...]
    rows, width = x.shape
    lane = lax.broadcasted_iota(jnp.int32, x.shape, 1)
    pos = lax.broadcasted_iota(jnp.int32, x.shape, 0) % CHUNK_A
    beta = jax.nn.sigmoid(x)
    z = x + par_ref[1:2, :]
    softplus = jnp.maximum(z, 0.0) + jnp.log1p(jnp.exp(-jnp.abs(z)))
    g = -jnp.exp(par_ref[0:1, :]) * softplus
    pre = g
    suf = g
    s = 1
    while s < CHUNK_A:
        pre = pre + jnp.where(pos >= s, pltpu.roll(pre, s, 0), 0.0)
        suf = suf + jnp.where(pos < CHUNK_A - s, pltpu.roll(suf, rows - s, 0), 0.0)
        s *= 2
    o_ref[...] = jnp.where(lane < 2 * H_A, beta, jnp.where(lane < 3 * H_A, pre, suf))


def _gates(ab, a_log, dt_bias):
    m = ab.shape[0]
    par = jnp.zeros((2, AUX_W), F32)
    par = par.at[0, 2 * H_A:4 * H_A].set(a_log.reshape(-1).astype(F32))
    par = par.at[1, 2 * H_A:4 * H_A].set(dt_bias.reshape(-1).astype(F32))
    blk = pl.BlockSpec((ROW_BLOCK, AUX_W), lambda i: (i, 0))
    return pl.pallas_call(
        _gates_kernel,
        out_shape=jax.ShapeDtypeStruct((m, AUX_W), F32),
        grid=(m // ROW_BLOCK,),
        in_specs=[blk, pl.BlockSpec((2, AUX_W), lambda i: (0, 0))],
        out_specs=blk,
        compiler_params=_cparams(("parallel",)),
        name="delta_gates",
    )(ab, par)


_HALO = 16


def _conv_kernel(prev_ref, cur_ref, next_ref, w_ref, o_ref, *, cb, nb, ncb, n_qk_blocks, n_q_blocks):
    t = pl.program_id(1)
    c = pl.program_id(2)
    prev_ok = jnp.logical_and(t != 0, t != ncb)
    next_ok = jnp.logical_and(t != ncb - 1, t != nb - 1)
    prev = jnp.where(prev_ok, prev_ref[0].astype(F32), 0.0)
    nxt = jnp.where(next_ok, next_ref[0].astype(F32), 0.0)
    x = jnp.concatenate([prev, cur_ref[0].astype(F32), nxt], axis=0)
    rows = x.shape[0]
    w = w_ref[...]
    y = None
    for d in range(CONV_W):
        shift = (CONV_W // 2 - d) % rows
        xs = x if shift == 0 else pltpu.roll(x, shift, 0)
        term = xs[_HALO:_HALO + ROW_BLOCK] * w[d:d + 1, :]
        y = term if y is None else y + term
    y = _silu(y)
    is_qk = c < n_qk_blocks
    q_scale = jnp.where(c < n_q_blocks, jnp.float32(DK_A ** -0.5), jnp.float32(1.0))
    for hh in range(cb // DK_A):
        s = y[:, hh * DK_A:(hh + 1) * DK_A]
        inv = lax.rsqrt(jnp.sum(s * s, axis=-1, keepdims=True) + EPS) * q_scale
        o_ref[0, :, hh * DK_A:(hh + 1) * DK_A] = (s * jnp.where(is_qk, inv, 1.0)).astype(o_ref.dtype)


def _conv_prep(p3, conv_w, nb, ncb, cb=512):
    b, lt, _ = p3.shape
    wc = conv_w.shape[1]
    per = ROW_BLOCK // _HALO
    n_halo = lt // _HALO
    kern = functools.partial(_conv_kernel, cb=cb, nb=nb, ncb=ncb,
                             n_qk_blocks=2 * H_A * DK_A // cb, n_q_blocks=H_A * DK_A // cb)
    return pl.pallas_call(
        kern,
        out_shape=jax.ShapeDtypeStruct((b, lt, wc), BF16),
        grid=(b, nb, wc // cb),
        in_specs=[
            pl.BlockSpec((1, _HALO, cb), lambda i, t, c: (i, jnp.maximum(t * per - 1, 0), c)),
            pl.BlockSpec((1, ROW_BLOCK, cb), lambda i, t, c: (i, t, c)),
            pl.BlockSpec((1, _HALO, cb), lambda i, t, c: (i, jnp.minimum((t + 1) * per, n_halo - 1), c)),
            pl.BlockSpec((CONV_W, cb), lambda i, t, c: (0, c)),
        ],
        out_specs=pl.BlockSpec((1, ROW_BLOCK, cb), lambda i, t, c: (i, t, c)),
        compiler_params=_cparams(("parallel", "parallel", "parallel")),
        name="conv_prep",
    )(p3, p3, p3, conv_w)


def _delta_prepare(items, wq_ref, u_ref, at_ref, kd_ref, ge_ref):
    c = CHUNK_A
    row = lax.broadcasted_iota(jnp.int32, (c, 2 * c), 0)
    col = lax.broadcasted_iota(jnp.int32, (c, 2 * c), 1) % c
    eye = (row == col).astype(F32)
    decay, t_inv, pw = [], [], []
    for (_, reverse, q, k, v, beta, gc, gc_row) in items:
        incl = (row <= col) if reverse else (row >= col)
        decay.append(jnp.exp(jnp.where(incl, gc - gc_row, -1e30)))
    k2 = [jnp.concatenate([it[3], it[3]], axis=0) for it in items]
    kk = [_dot_nt(it[3], kd) for it, kd in zip(items, k2)]
    qk = [_dot_nt(it[2], kd) for it, kd in zip(items, k2)]
    for i, (_, reverse, q, k, v, beta, gc, gc_row) in enumerate(items):
        strict = (row < col) if reverse else (row > col)
        n = jnp.where(strict, beta * kk[i] * decay[i], 0.0)
        pw.append(n)
        t_inv.append(eye - n)
    pw = [_dot3(p, p) for p in pw]
    span = 4
    while span < c:
        prod = [_dot3(jnp.concatenate([p, t], axis=0), p) for p, t in zip(pw, t_inv)]
        pw = [r[0:c] for r in prod]
        t_inv = [t + r[c:2 * c] for t, r in zip(t_inv, prod)]
        span *= 2
    t_inv = [t + _dot3(t, p) for t, p in zip(t_inv, pw)]
    rhs, e_gc = [], []
    for (_, reverse, q, k, v, beta, gc, gc_row) in items:
        e = jnp.exp(gc)
        e_gc.append(e)
        r = jnp.concatenate([v.astype(F32) * beta, k.astype(F32) * (beta * e)], axis=-1).astype(BF16)
        rhs.append(jnp.concatenate([r, r], axis=0))
    uw = [_dot(_hilo_lanes(t), r) for t, r in zip(t_inv, rhs)]
    for i, (slot, reverse, q, k, v, beta, gc, gc_row) in enumerate(items):
        last = 0 if reverse else c - 1
        g_last = gc[last:last + 1, :]
        u_ref[slot] = uw[i][:, :DV_A]
        wq_ref[slot, 0:c, :] = uw[i][:, DV_A:].astype(BF16)
        wq_ref[slot, c:2 * c, :] = (q.astype(F32) * e_gc[i]).astype(BF16)
        at_ref[slot] = (qk[i] * decay[i])[:, 0:c].astype(BF16)
        kd_ref[slot] = (k.astype(F32) * jnp.exp(g_last - gc)).astype(BF16)
        ge_ref[slot] = jnp.broadcast_to(jnp.exp(g_last), ge_ref.shape[1:])


def _delta_kernel(qf_ref, kf_ref, vf_ref, gcf_ref, grf_ref, qb_ref, kb_ref, vb_ref, gcb_ref, grb_ref,
                  of_ref, ob_ref, s_ref, wq_ref, u_ref, at_ref, kd_ref, ge_ref, *, heads):
    @pl.when(pl.program_id(2) == 0)
    def _():
        s_ref[...] = jnp.zeros_like(s_ref)

    c = CHUNK_A
    nchunks = ROW_BLOCK // c
    fwd = (qf_ref, kf_ref, vf_ref, gcf_ref, grf_ref, of_ref)
    bwd = (qb_ref, kb_ref, vb_ref, gcb_ref, grb_ref, ob_ref)

    def slot(hh, reverse, cc):
        return (hh * 2 + int(reverse)) * nchunks + cc

    for h0 in range(0, heads, DELTA_WAVE):
        items = []
        for hh in range(h0, min(h0 + DELTA_WAVE, heads)):
            for reverse in (False, True):
                q_ref, k_ref, v_ref, gc_ref, gr_ref, _ = bwd if reverse else fwd
                bcol, gcol = (1, 3) if reverse else (0, 2)
                hs = slice(hh * DK_A, (hh + 1) * DK_A)
                for cc in range(nchunks):
                    rows = slice(cc * c, (cc + 1) * c)
                    gcols = gc_ref[0, hh, rows, :]
                    items.append((slot(hh, reverse, cc), reverse, q_ref[0, rows, hs], k_ref[0, rows, hs],
                                  v_ref[0, rows, hs], gcols[:, bcol:bcol + 1], gcols[:, gcol:gcol + 1],
                                  gr_ref[0, hh, cc, gcol:gcol + 1, :]))
        _delta_prepare(items, wq_ref, u_ref, at_ref, kd_ref, ge_ref)

    chains = [(hh, reverse) for hh in range(heads) for reverse in (False, True)]
    states = [s_ref[hh * 2 + int(reverse)] for hh, reverse in chains]
    for ci in range(nchunks):
        slots = [slot(hh, reverse, nchunks - 1 - ci if reverse else ci) for hh, reverse in chains]
        sbs = [s.astype(BF16) for s in states]
        ws = [_dot(wq_ref[sl], sb) for sl, sb in zip(slots, sbs)]
        vns = [(u_ref[sl] - r[0:c]).astype(BF16) for sl, r in zip(slots, ws)]
        intra = [_dot(at_ref[sl], vn) for sl, vn in zip(slots, vns)]
        upd = [_dot_tn(kd_ref[sl], vn) for sl, vn in zip(slots, vns)]
        for j, (hh, reverse) in enumerate(chains):
            cc = nchunks - 1 - ci if reverse else ci
            o_ref = bwd[5] if reverse else fwd[5]
            o_ref[0, cc * c:(cc + 1) * c, hh * DV_A:(hh + 1) * DV_A] = (ws[j][c:2 * c] + intra[j]).astype(o_ref.dtype)
            states[j] = states[j] * ge_ref[slots[j], 0:1, :] + upd[j]
    for j, (hh, reverse) in enumerate(chains):
        s_ref[hh * 2 + int(reverse)] = states[j]


def _bwd_block(t, nb, ncb):
    return jnp.where(t < ncb, ncb - 1 - t, nb - 1 - (t - ncb))


def _delta_scan(qkv, g_col, g_row, nb, ncb):
    b, lt, _ = qkv.shape
    nch = ROW_BLOCK // CHUNK_A
    hg = _pick(H_A, (DELTA_HEADS, 2, 1))
    ngrp = H_A // hg
    n_slots = hg * 2 * nch

    def blk(rev, t):
        return _bwd_block(t, nb, ncb) if rev else t

    def tok(grp0, rev):
        return pl.BlockSpec((1, ROW_BLOCK, hg * DK_A), lambda i, h, t: (i, blk(rev, t), grp0 + h))

    def gcol(rev):
        return pl.BlockSpec((1, hg, ROW_BLOCK, 4), lambda i, h, t: (i, h, blk(rev, t), 0))

    def grow(rev):
        return pl.BlockSpec((1, hg, nch, 4, 2 * CHUNK_A), lambda i, h, t: (i, h, blk(rev, t), 0, 0))

    in_specs, args = [], []
    for rev in (False, True):
        in_specs += [tok(0, rev), tok(ngrp, rev), tok(2 * ngrp, rev), gcol(rev), grow(rev)]
        args += [qkv, qkv, qkv, g_col, g_row]
    o_shape = jax.ShapeDtypeStruct((b, lt, H_A * DV_A), BF16)
    return pl.pallas_call(
        functools.partial(_delta_kernel, heads=hg),
        out_shape=[o_shape, o_shape],
        grid=(b, ngrp, nb),
        in_specs=in_specs,
        out_specs=[tok(0, False), tok(0, True)],
        scratch_shapes=[
            pltpu.VMEM((hg * 2, DK_A, DV_A), F32),
            pltpu.VMEM((n_slots, 2 * CHUNK_A, DK_A), BF16),
            pltpu.VMEM((n_slots, CHUNK_A, DV_A), F32),
            pltpu.VMEM((n_slots, CHUNK_A, CHUNK_A), BF16),
            pltpu.VMEM((n_slots, CHUNK_A, DK_A), BF16),
            pltpu.VMEM((n_slots, 8, LANES), F32),
        ],
        compiler_params=_cparams(("parallel", "parallel", "arbitrary")),
        name="delta_scan",
    )(*args)


def _delta_out_kernel(of_ref, ob_ref, z_ref, w_ref, o_ref, *, cb):
    o = of_ref[...].astype(F32) + ob_ref[...].astype(F32)
    z = z_ref[...].astype(F32)
    w = w_ref[...]
    for hh in range(cb // DV_A):
        sl = slice(hh * DV_A, (hh + 1) * DV_A)
        s = o[:, sl]
        y = s * lax.rsqrt(jnp.mean(s * s, axis=-1, keepdims=True) + EPS) * w
        o_ref[:, sl] = (y * _silu(z[:, sl])).astype(o_ref.dtype)


def _delta_out(o_f, o_b, p, col_z, dn_norm_w, cb=512):
    m, wa = o_f.shape
    blk = pl.BlockSpec((ROW_BLOCK, cb), lambda i, c: (i, c))
    j0 = col_z // cb
    return pl.pallas_call(
        functools.partial(_delta_out_kernel, cb=cb),
        out_shape=jax.ShapeDtypeStruct((m, wa), BF16),
        grid=(m // ROW_BLOCK, wa // cb),
        in_specs=[blk, blk, pl.BlockSpec((ROW_BLOCK, cb), lambda i, c: (i, j0 + c)),
                  pl.BlockSpec((1, DV_A), lambda i, c: (0, 0))],
        out_specs=blk,
        compiler_params=_cparams(("parallel", "parallel")),
        name="delta_out",
    )(o_f, o_b, p, dn_norm_w.reshape(1, DV_A).astype(F32))


def _rotate(t, cos, sin):
    half = t.shape[-1] // 2
    t1, t2 = t[:, :half], t[:, half:]
    return jnp.concatenate([t1 * cos - t2 * sin, t2 * cos + t1 * sin], axis=-1)


def _ret_kernel(qf_ref, kf_ref, vf_ref, cf_ref, sf_ref, qb_ref, kb_ref, vb_ref, cb_ref, sb_ref, lg_ref,
                of_ref, ob_ref, stf_ref, stb_ref):
    @pl.when(pl.program_id(2) == 0)
    def _():
        stf_ref[...] = jnp.zeros_like(stf_ref)
        stb_ref[...] = jnp.zeros_like(stb_ref)

    c = ROW_BLOCK
    k_scale = DK_B ** -0.5
    row = lax.broadcasted_iota(jnp.int32, (c, c), 0)
    col = lax.broadcasted_iota(jnp.int32, (c, c), 1)
    pos0 = lax.broadcasted_iota(jnp.int32, (c, 1), 0)
    dirs = ((False, qf_ref, kf_ref, vf_ref, cf_ref, sf_ref, of_ref, stf_ref),
            (True, qb_ref, kb_ref, vb_ref, cb_ref, sb_ref, ob_ref, stb_ref))
    qs, ks, lgs, posfs, decays = [], [], [], [], []
    for reverse, q_ref, k_ref, v_ref, c_ref, s_ref, o_ref, st_ref in dirs:
        cos, sin = c_ref[...], s_ref[...]
        qs.append(_rotate(q_ref[0].astype(F32), cos, sin))
        ks.append(_rotate(k_ref[0].astype(F32) * k_scale, cos, sin))
        lg = lg_ref[0, 1 if reverse else 0, 0:1, 0:1]
        lgs.append(lg)
        dist = (col - row) if reverse else (row - col)
        decays.append(jnp.exp(jnp.where(dist >= 0, dist.astype(F32) * lg, -1e30)))
        posfs.append(((c - 1 - pos0) if reverse else pos0).astype(F32))
    vs = [d[3][0] for d in dirs]
    states = [d[7][...] for d in dirs]
    qk = [_dot_nt(q.astype(BF16), k.astype(BF16)) for q, k in zip(qs, ks)]
    inter = [_dot((q * jnp.exp((p + 1.0) * lg)).astype(BF16), s.astype(BF16))
             for q, p, lg, s in zip(qs, posfs, lgs, states)]
    upd = [_dot_tn((k * jnp.exp((c - 1.0 - p) * lg)).astype(BF16), v)
           for k, p, lg, v in zip(ks, posfs, lgs, vs)]
    intra = [_dot((a * dec).astype(BF16), v) for a, dec, v in zip(qk, decays, vs)]
    for i, d in enumerate(dirs):
        d[6][0] = (intra[i] + inter[i]).astype(d[6].dtype)
        d[7][...] = states[i] * jnp.exp(c * lgs[i]) + upd[i]


def _ret_scan(p3, col_q, col_k, col_v, cos, sin, lg, nb, ncb):
    b, lt, _ = p3.shape
    jq, jk, jv = col_q // DK_B, col_k // DK_B, col_v // DV_B

    def tok(j0, rev):
        if rev:
            return pl.BlockSpec((1, ROW_BLOCK, DK_B), lambda i, h, t: (i, _bwd_block(t, nb, ncb), j0 + h))
        return pl.BlockSpec((1, ROW_BLOCK, DK_B), lambda i, h, t: (i, t, j0 + h))

    def ang(rev):
        if rev:
            return pl.BlockSpec((ROW_BLOCK, DK_B // 2), lambda i, h, t: (_bwd_block(t, nb, ncb), 0))
        return pl.BlockSpec((ROW_BLOCK, DK_B // 2), lambda i, h, t: (t, 0))

    in_specs, args = [], []
    for rev in (False, True):
        in_specs += [tok(jq, rev), tok(jk, rev), tok(jv, rev), ang(rev), ang(rev)]
        args += [p3, p3, p3, cos, sin]
    in_specs.append(pl.BlockSpec((1, 2, 8, LANES), lambda i, h, t: (h, 0, 0, 0)))
    args.append(lg)
    o_shape = jax.ShapeDtypeStruct((b, lt, H_B * DV_B), BF16)
    return pl.pallas_call(
        _ret_kernel,
        out_shape=[o_shape, o_shape],
        grid=(b, H_B, nb),
        in_specs=in_specs,
        out_specs=[tok(0, False), tok(0, True)],
        scratch_shapes=[pltpu.VMEM((DK_B, DV_B), F32), pltpu.VMEM((DK_B, DV_B), F32)],
        compiler_params=_cparams(("parallel", "parallel", "arbitrary")),
        name="retention_scan",
    )(*args)


def _ret_out_kernel(of_ref, ob_ref, g_ref, w_ref, b_ref, o_ref, *, cb):
    o = of_ref[...].astype(F32) + ob_ref[...].astype(F32)
    g = g_ref[...].astype(F32)
    for hh in range(cb // DV_B):
        sl = slice(hh * DV_B, (hh + 1) * DV_B)
        s = o[:, sl]
        d = s - jnp.mean(s, axis=-1, keepdims=True)
        y = d * lax.rsqrt(jnp.mean(d * d, axis=-1, keepdims=True) + GN_EPS)
        o_ref[:, sl] = ((y * w_ref[:, sl] + b_ref[:, sl]) * _silu(g[:, sl])).astype(o_ref.dtype)


def _ret_out(o_f, o_b, p, col_g, gn_w, gn_b, cb=512):
    m, wb = o_f.shape
    blk = pl.BlockSpec((ROW_BLOCK, cb), lambda i, c: (i, c))
    par = pl.BlockSpec((1, cb), lambda i, c: (0, c))
    j0 = col_g // cb
    return pl.pallas_call(
        functools.partial(_ret_out_kernel, cb=cb),
        out_shape=jax.ShapeDtypeStruct((m, wb), BF16),
        grid=(m // ROW_BLOCK, wb // cb),
        in_specs=[blk, blk, pl.BlockSpec((ROW_BLOCK, cb), lambda i, c: (i, j0 + c)), par, par],
        out_specs=blk,
        compiler_params=_cparams(("parallel", "parallel")),
        name="retention_out",
    )(o_f, o_b, p, gn_w.reshape(1, wb).astype(F32), gn_b.reshape(1, wb).astype(F32))


def _rope_tables(ctx_len, seq):
    n_freq = DK_B // 4
    freqs = ROPE_BASE ** (-jnp.arange(n_freq, dtype=F32) / n_freq)
    rows = seq // GRID_W
    r = jnp.repeat(jnp.arange(rows, dtype=F32), GRID_W)
    col = jnp.tile(jnp.arange(GRID_W, dtype=F32), rows)
    ang = jnp.concatenate([r[:, None] * freqs, col[:, None] * freqs], axis=-1)
    ang = jnp.concatenate([jnp.zeros((ctx_len, DK_B // 2), F32), ang], axis=0)
    return jnp.cos(ang), jnp.sin(ang)


def _retention_log_decays():
    h = jnp.arange(H_B, dtype=F32)
    fwd = jnp.log1p(-jnp.exp2(-5.0 - h))
    bwd = jnp.log1p(-jnp.exp2(-5.5 - h))
    lg = jnp.stack([fwd, bwd], axis=1)
    return jnp.broadcast_to(lg[:, :, None, None], (H_B, 2, 8, LANES))


def kernel(x, c, ctx, c_ctx, mod_down, mod_up, mod_bias, norm_gains, w_in, conv_w, a_log, dt_bias, dn_norm_w,
           gn_w, gn_b, w_br_a, w_br_b, w_out, ffn_gate, ffn_up, ffn_down, router, moe_gate, moe_up, moe_down):
    b, seq, d = x.shape
    ctx_len = ctx.shape[1]
    depth = w_in.shape[0]
    lt = ctx_len + seq
    assert ctx_len % ROW_BLOCK == 0 and seq % ROW_BLOCK == 0 and seq % GRID_W == 0 and b + 1 <= 8
    nb, ncb = lt // ROW_BLOCK, ctx_len // ROW_BLOCK
    m = b * lt
    w_conv = 2 * H_A * DK_A + H_A * DV_A
    w_a, w_b = H_A * DV_A, H_B * DV_B
    n_ab = 4 * H_A
    col_z = w_conv
    n_p1 = col_z + w_a
    col_kb = H_B * DK_B
    col_vb = col_kb + H_B * DK_B
    col_gb = col_vb + w_b
    col_ga = col_gb + w_b
    col_gate_b = col_ga + d
    n_p2 = col_gate_b + d

    tm = _pick(m, (768, 512, 256))

    cond = jnp.zeros((8, d), F32).at[:b].set(c).at[b].set(c_ctx)
    mod = _modulation(cond, mod_down, mod_up, mod_bias)[:, :b + 1].reshape(depth, b + 1, 6, d)

    cos, sin = _rope_tables(ctx_len, seq)
    lg = _retention_log_decays()

    w_br_a, w_br_b, w_out = _to_bf16(w_br_a), _to_bf16(w_br_b), _to_bf16(w_out)
    ffn_gate, ffn_up, ffn_down = _to_bf16(ffn_gate), _to_bf16(ffn_up), _to_bf16(ffn_down)
    moe_gate, moe_up, moe_down = _to_bf16(moe_gate), _to_bf16(moe_up), _to_bf16(moe_down)

    def in_weights(i):
        w = w_in[i]
        w_ab = jnp.zeros((d, AUX_W), BF16).at[:, :n_ab].set(w[:, n_p1:n_p1 + n_ab].astype(BF16))
        return w[:, :n_p1].astype(BF16), w[:, n_p1 + n_ab:].astype(BF16), w_ab

    def mix_params(i, idx):
        return jnp.stack([mod[i, :, idx[0]], mod[i, :, idx[1]], mod[i, :, idx[2]]], axis=1)

    w_p1, w_p2, w_ab = in_weights(0)
    mp0 = jnp.stack([jnp.zeros_like(mod[0, :, 0]), mod[0, :, 0], mod[0, :, 1]], axis=1)
    xs, h, ab = _row_stage(x.reshape(b * seq, d), None, mp0, jnp.stack([norm_gains[0, 0], norm_gains[0, 0]]),
                           w_ab, nb, ncb, has_h=True, ctx_rows=ctx.reshape(b * ctx_len, d))

    for i in range(depth):
        gains = norm_gains[i]
        tn_in = (1024, 512, 256, 128)
        p = _matmul(h, w_p1[None], 0, tm=tm, tn=_pick(n_p1, tn_in), name="in_proj_a")
        p2 = _matmul(h, w_p2[None], 0, tm=tm, tn=_pick(n_p2, tn_in), name="in_proj_b")
        p3 = p.reshape(b, lt, n_p1)
        p23 = p2.reshape(b, lt, n_p2)
        g = _gates(ab, a_log[i], dt_bias[i])[:, :n_ab].reshape(b, lt, 4, H_A)
        g_col = g.transpose(0, 3, 1, 2)
        g_row = g.reshape(b, lt // CHUNK_A, CHUNK_A, 4, H_A).transpose(0, 4, 1, 3, 2)
        g_row = jnp.concatenate([g_row, g_row], axis=-1)
        qkv = _conv_prep(p3, conv_w[i], nb, ncb, cb=_pick(H_A * DK_A, (512, 256, 128)))
        of, ob = _delta_scan(qkv, g_col, g_row, nb, ncb)
        o_a = _delta_out(of.reshape(m, w_a), ob.reshape(m, w_a), p, col_z, dn_norm_w[i],
                         cb=_pick(w_a, (512, 256, 128)))
        rf, rb = _ret_scan(p23, 0, col_kb, col_vb, cos, sin, lg, nb, ncb)
        o_b = _ret_out(rf.reshape(m, w_b), rb.reshape(m, w_b), p2, col_gb, gn_w[i], gn_b[i],
                       cb=_pick(w_b, (512, 256)))
        tn_d = _pick(d, (1024, 512, 256, 128))
        merged = _branch_merge(o_a, w_br_a, o_b, w_br_b, i, p2, col_ga, col_gate_b,
                               tm=tm, tn=_pick(d, (512, 256, 128)))
        y = _matmul(merged, w_out, i, tm=tm, tn=tn_d, name="out_proj")
        j = i // 2
        is_moe = i % 2 == 1
        gn = jnp.stack([gains[1], gains[2]])
        if is_moe:
            n_exp = router.shape[2]
            w_r = jnp.zeros((d, AUX_W), BF16).at[:, :n_exp].set(router[j].astype(BF16))
            xs, h2, logits = _row_stage(xs, y, mix_params(i, (2, 3, 4)), gn, w_r, nb, ncb, has_h=True)
            combine = _router(logits, n_exp)
            f_e = moe_gate.shape[3]
            wg = moe_gate.reshape(-1, d, f_e)
            wu = moe_up.reshape(-1, d, f_e)
            wd = moe_down.reshape(-1, n_exp * f_e, d)
        else:
            xs, h2 = _row_stage(xs, y, mix_params(i, (2, 3, 4)), gn, None, nb, ncb, has_h=True)
            combine = None
            n_exp = 1
            wg, wu, wd = ffn_gate, ffn_up, ffn_down
        act = _ffn_up(h2, wg, wu, j, n_exp, combine, tm=tm, tn=_pick(wg.shape[2], (512, 256, 128)))
        y2 = _matmul(act, wd, j, tm=tm, tn=_pick(d, (512, 256, 128)), name="ffn_down")
        if i + 1 < depth:
            w_p1, w_p2, w_ab = in_weights(i + 1)
            mp = jnp.stack([mod[i, :, 5], mod[i + 1, :, 0], mod[i + 1, :, 1]], axis=1)
            gn = jnp.stack([gains[3], norm_gains[i + 1, 0]])
            xs, h, ab = _row_stage(xs, y2, mp, gn, w_ab, nb, ncb, has_h=True)
        else:
            mp = jnp.stack([mod[i, :, 5], mod[i, :, 5], mod[i, :, 5]], axis=1)
            (out,) = _row_stage(xs, y2, mp, jnp.stack([gains[3], gains[3]]), None, nb, ncb, has_h=False,
                                latent_only=True)

    return out.reshape(b, seq, d)
```

```python
import functools

import jax
import jax.numpy as jnp
from jax import lax
from jax.experimental import pallas as pl
from jax.experimental.pallas import tpu as pltpu

GRID_W = 64
CONV_W = 5
H_A = 16
DK_A = 128
DV_A = 128
H_B = 8
DK_B = 256
DV_B = 256
TOP_K = 2
ROPE_BASE = 10000.0
EPS = 1e-6
GN_EPS = 1e-5

LANES = 128
ROW_BLOCK = 256
CHUNK_A = 64
DELTA_HEADS = 8
DELTA_WAVE = 2
AUX_W = LANES
VMEM_LIMIT = 56 * 1024 * 1024

F32 = jnp.float32
BF16 = jnp.bfloat16


def _cparams(sem):
    return pltpu.CompilerParams(dimension_semantics=sem, vmem_limit_bytes=VMEM_LIMIT)


def _silu(t):
    return t * jax.nn.sigmoid(t)


def _dot(a, b):
    return jnp.dot(a, b, preferred_element_type=F32)


def _dot_nt(a, b):
    return lax.dot_general(a, b, (((1,), (1,)), ((), ())), preferred_element_type=F32)


def _dot_tn(a, b):
    return lax.dot_general(a, b, (((0,), (0,)), ((), ())), preferred_element_type=F32)


def _hilo_lanes(a):
    hif = a.astype(BF16).astype(F32)
    lane = lax.broadcasted_iota(jnp.int32, a.shape, 1)
    return jnp.where(lane < a.shape[1] // 2, hif, a - hif).astype(BF16)


def _dot3(a, b):
    bh = b.astype(BF16)
    bl = (b - bh.astype(F32)).astype(BF16)
    lhs = jnp.concatenate([_hilo_lanes(a), a.astype(BF16)], axis=1)
    rhs = jnp.concatenate([bh, bh, bl, jnp.zeros_like(bh)], axis=0)
    return _dot(lhs, rhs)


def _mod_kernel(cond_ref, wd_ref, wu_ref, b_ref, o_ref, t_ref):
    @pl.when(pl.program_id(1) == 0)
    def _():
        s = _silu(cond_ref[...])
        t_ref[...] = _dot(s.astype(BF16), wd_ref[0].astype(BF16))

    o_ref[0] = _dot(t_ref[...].astype(BF16), wu_ref[0].astype(BF16)) + b_ref[0]


def _modulation(cond, w_down, w_up, bias):
    depth, d, r = w_down.shape
    n = w_up.shape[2]
    tn = _pick(n, (2048, 1024, 512, 256, 128))
    return pl.pallas_call(
        _mod_kernel,
        out_shape=jax.ShapeDtypeStruct((depth, 8, n), F32),
        grid=(depth, n // tn),
        in_specs=[
            pl.BlockSpec((8, d), lambda l, j: (0, 0)),
            pl.BlockSpec((1, d, r), lambda l, j: (l, 0, 0)),
            pl.BlockSpec((1, r, tn), lambda l, j: (l, 0, j)),
            pl.BlockSpec((1, 1, tn), lambda l, j: (l, 0, j)),
        ],
        out_specs=pl.BlockSpec((1, 8, tn), lambda l, j: (l, 0, j)),
        scratch_shapes=[pltpu.VMEM((8, r), F32)],
        compiler_params=_cparams(("arbitrary", "arbitrary")),
        name="modulation",
    )(cond, w_down, w_up, bias.reshape(depth, 1, n))


def _rms(t, gain):
    return t * lax.rsqrt(jnp.mean(t * t, axis=-1, keepdims=True) + EPS) * gain


def _row_kernel(*refs, has_y, has_h, has_aux, nb, ncb, split_in):
    it = iter(refs)
    c_ref = next(it) if split_in else None
    x_ref = next(it)
    y_ref = next(it) if has_y else None
    mp_ref = next(it)
    gn_ref = next(it)
    waux_ref = next(it) if has_aux else None
    xo_ref = next(it) if (has_y or split_in) else None
    h_ref = next(it) if has_h else None
    aux_ref = next(it) if has_aux else None

    x = x_ref[...]
    if split_in:
        x = jnp.where(pl.program_id(0) % nb < ncb, c_ref[...], x)
    if has_y:
        x = x + mp_ref[0, 0:1, :] * _rms(y_ref[...].astype(F32), gn_ref[0:1, :])
    if xo_ref is not None:
        xo_ref[...] = x
    if has_h:
        h = _rms(x, gn_ref[1:2, :]) * (1.0 + mp_ref[0, 2:3, :]) + mp_ref[0, 1:2, :]
        hb = h.astype(BF16)
        h_ref[...] = hb
        if has_aux:
            aux_ref[...] = _dot(hb, waux_ref[...])


def _row_stage(x, y, mp, gn, waux, nb, ncb, *, has_h, ctx_rows=None, latent_only=False):
    d = x.shape[1]
    has_y = y is not None
    has_aux = waux is not None
    split_in = ctx_rows is not None
    n_ctx_rows = mp.shape[0] - 1
    nlb = nb - ncb
    m = x.shape[0] + ctx_rows.shape[0] if split_in else x.shape[0]
    steps_per_batch = nlb if latent_only else nb
    first = ncb if latent_only else 0

    def full(i):
        return (i // steps_per_batch) * nb + first + i % steps_per_batch

    def sel(i):
        return jnp.where(full(i) % nb < ncb, n_ctx_rows, full(i) // nb)

    row = pl.BlockSpec((ROW_BLOCK, d), lambda i: (full(i), 0))
    out_row = pl.BlockSpec((ROW_BLOCK, d), lambda i: (i, 0))
    in_specs, args = [], []
    if split_in:
        in_specs += [pl.BlockSpec((ROW_BLOCK, d), lambda i: ((i // nb) * ncb + jnp.minimum(i % nb, ncb - 1), 0)),
                     pl.BlockSpec((ROW_BLOCK, d), lambda i: ((i // nb) * nlb + jnp.maximum(i % nb - ncb, 0), 0))]
        args += [ctx_rows, x]
    else:
        in_specs.append(row)
        args.append(x)
    if has_y:
        in_specs.append(row)
        args.append(y)
    in_specs += [pl.BlockSpec((1, 3, d), lambda i: (sel(i), 0, 0)), pl.BlockSpec((2, d), lambda i: (0, 0))]
    args += [mp, gn]
    if has_aux:
        in_specs.append(pl.BlockSpec((d, AUX_W), lambda i: (0, 0)))
        args.append(waux)
    n_out_rows = (m // nb) * nlb if latent_only else m
    out_shape, out_specs = [], []
    if has_y or split_in:
        out_shape.append(jax.ShapeDtypeStruct((n_out_rows, d), F32))
        out_specs.append(out_row)
    if has_h:
        out_shape.append(jax.ShapeDtypeStruct((n_out_rows, d), BF16))
        out_specs.append(out_row)
    if has_aux:
        out_shape.append(jax.ShapeDtypeStruct((n_out_rows, AUX_W), F32))
        out_specs.append(pl.BlockSpec((ROW_BLOCK, AUX_W), lambda i: (i, 0)))
    outs = pl.pallas_call(
        functools.partial(_row_kernel, has_y=has_y, has_h=has_h, has_aux=has_aux, nb=nb, ncb=ncb,
                          split_in=split_in),
        out_shape=out_shape,
        grid=(n_out_rows // ROW_BLOCK,),
        in_specs=in_specs,
        out_specs=out_specs,
        compiler_params=_cparams(("parallel",)),
        name="row_stage",
    )(*args)
    return list(outs)


def _cast_kernel(w_ref, o_ref):
    o_ref[...] = w_ref[...].astype(o_ref.dtype)


def _to_bf16(w):
    w2 = w.reshape(-1, w.shape[-1])
    r, c = w2.shape
    tr = _pick(r, (512, 256, 128, 64, 32, 16))
    tc = _pick(c, (2048, 1024, 512, 256, 128))
    blk = pl.BlockSpec((tr, tc), lambda i, j: (i, j))
    out = pl.pallas_call(
        _cast_kernel,
        out_shape=jax.ShapeDtypeStruct((r, c), BF16),
        grid=(r // tr, c // tc),
        in_specs=[blk],
        out_specs=blk,
        compiler_params=_cparams(("parallel", "parallel")),
        name="weights_to_bf16",
    )(w2)
    return out.reshape(w.shape)


def _mm_kernel(a_ref, w_ref, o_ref):
    o_ref[...] = _dot(a_ref[...], w_ref[0]).astype(o_ref.dtype)


def _pick(n, pref):
    for t in pref:
        if n % t == 0:
            return t
    return n


def _matmul(a, w, layer, *, tm, tn, out_dtype=BF16, name="matmul"):
    m, k = a.shape
    n = w.shape[2]
    return pl.pallas_call(
        _mm_kernel,
        out_shape=jax.ShapeDtypeStruct((m, n), out_dtype),
        grid=(m // tm, n // tn),
        in_specs=[pl.BlockSpec((tm, k), lambda i, j: (i, 0)), pl.BlockSpec((1, k, tn), lambda i, j: (layer, 0, j))],
        out_specs=pl.BlockSpec((tm, tn), lambda i, j: (i, j)),
        compiler_params=_cparams(("parallel", "parallel")),
        name=name,
    )(a, w)


def _merge_kernel(oa_ref, wa_ref, ob_ref, wb_ref, ga_ref, gb_ref, o_ref):
    ya = _dot(oa_ref[...], wa_ref[0])
    yb = _dot(ob_ref[...], wb_ref[0])
    ga = jax.nn.sigmoid(ga_ref[...].astype(F32))
    gb = jax.nn.sigmoid(gb_ref[...].astype(F32))
    o_ref[...] = (ga * ya + gb * yb).astype(o_ref.dtype)


def _branch_merge(o_a, w_a, o_b, w_b, layer, p, col_ga, col_gb, *, tm, tn):
    m, ka = o_a.shape
    kb = o_b.shape[1]
    n = w_a.shape[2]
    ja, jb = col_ga // tn, col_gb // tn
    return pl.pallas_call(
        _merge_kernel,
        out_shape=jax.ShapeDtypeStruct((m, n), BF16),
        grid=(m // tm, n // tn),
        in_specs=[
            pl.BlockSpec((tm, ka), lambda i, j: (i, 0)),
            pl.BlockSpec((1, ka, tn), lambda i, j: (layer, 0, j)),
            pl.BlockSpec((tm, kb), lambda i, j: (i, 0)),
            pl.BlockSpec((1, kb, tn), lambda i, j: (layer, 0, j)),
            pl.BlockSpec((tm, tn), lambda i, j: (i, ja + j)),
            pl.BlockSpec((tm, tn), lambda i, j: (i, jb + j)),
        ],
        out_specs=pl.BlockSpec((tm, tn), lambda i, j: (i, j)),
        compiler_params=_cparams(("parallel", "parallel")),
        name="branch_merge",
    )(o_a, w_a, o_b, w_b, p, p)


def _ffn_up_kernel(*refs, tn, f_expert, has_combine):
    if has_combine:
        h_ref, wg_ref, wu_ref, cmb_ref, o_ref = refs
    else:
        h_ref, wg_ref, wu_ref, o_ref = refs
    h = h_ref[...]
    act = _silu(_dot(h, wg_ref[0])) * _dot(h, wu_ref[0])
    if has_combine:
        e = (pl.program_id(1) * tn) // f_expert
        cmb = cmb_ref[...]
        lane = lax.broadcasted_iota(jnp.int32, cmb.shape, 1)
        act = act * jnp.sum(jnp.where(lane == e, cmb, 0.0), axis=-1, keepdims=True)
    o_ref[...] = act.astype(o_ref.dtype)


def _ffn_up(h, w_gate, w_up, layer, n_exp, combine, *, tm, tn):
    m, k = h.shape
    f_expert = w_gate.shape[2]
    n = n_exp * f_expert
    per = f_expert // tn
    has_combine = combine is not None
    wspec = pl.BlockSpec((1, k, tn), lambda i, j: (layer * n_exp + j // per, 0, j % per))
    in_specs = [pl.BlockSpec((tm, k), lambda i, j: (i, 0)), wspec, wspec]
    args = [h, w_gate, w_up]
    if has_combine:
        in_specs.append(pl.BlockSpec((tm, AUX_W), lambda i, j: (i, 0)))
        args.append(combine)
    return pl.pallas_call(
        functools.partial(_ffn_up_kernel, tn=tn, f_expert=f_expert, has_combine=has_combine),
        out_shape=jax.ShapeDtypeStruct((m, n), BF16),
        grid=(m // tm, n // tn),
        in_specs=in_specs,
        out_specs=pl.BlockSpec((tm, tn), lambda i, j: (i, j)),
        compiler_params=_cparams(("parallel", "parallel")),
        name="ffn_up",
    )(*args)


def _router_kernel(lg_ref, o_ref, *, n_experts):
    lg = lg_ref[...]
    lane = lax.broadcasted_iota(jnp.int32, lg.shape, 1)
    neg = jnp.float32(-jnp.inf)
    valid = lane < n_experts
    v = jnp.where(valid, lg, neg)
    m1 = jnp.max(v, axis=-1, keepdims=True)
    i1 = jnp.min(jnp.where(v == m1, lane, AUX_W), axis=-1, keepdims=True)
    v2 = jnp.where(lane == i1, neg, v)
    m2 = jnp.max(v2, axis=-1, keepdims=True)
    i2 = jnp.min(jnp.where(v2 == m2, lane, AUX_W), axis=-1, keepdims=True)
    e2 = jnp.exp(m2 - m1)
    p1 = 1.0 / (1.0 + e2)
    p2 = e2 / (1.0 + e2)
    o_ref[...] = jnp.where(lane == i1, p1, 0.0) + jnp.where(lane == i2, p2, 0.0)


def _router(logits, n_experts):
    m = logits.shape[0]
    blk = pl.BlockSpec((ROW_BLOCK, AUX_W), lambda i: (i, 0))
    return pl.pallas_call(
        functools.partial(_router_kernel, n_experts=n_experts),
        out_shape=jax.ShapeDtypeStruct((m, AUX_W), F32),
        grid=(m // ROW_BLOCK,),
        in_specs=[blk],
        out_specs=blk,
        compiler_params=_cparams(("parallel",)),
        name="router_top2",
    )(logits)


def _gates_kernel(ab_ref, par_ref, o_ref):
    x = ab_ref[...]
    rows, width = x.shape
    lane = lax.broadcasted_iota(jnp.int32, x.shape, 1)
    pos = lax.broadcasted_iota(jnp.int32, x.shape, 0) % CHUNK_A
    beta = jax.nn.sigmoid(x)
    z = x + par_ref[1:2, :]
    softplus = jnp.maximum(z, 0.0) + jnp.log1p(jnp.exp(-jnp.abs(z)))
    g = -jnp.exp(par_ref[0:1, :]) * softplus
    pre = g
    suf = g
    s = 1
    while s < CHUNK_A:
        pre = pre + jnp.where(pos >= s, pltpu.roll(pre, s, 0), 0.0)
        suf = suf + jnp.where(pos < CHUNK_A - s, pltpu.roll(suf, rows - s, 0), 0.0)
        s *= 2
    o_ref[...] = jnp.where(lane < 2 * H_A, beta, jnp.where(lane < 3 * H_A, pre, suf))


def _gates(ab, a_log, dt_bias):
    m = ab.shape[0]
    par = jnp.zeros((2, AUX_W), F32)
    par = par.at[0, 2 * H_A:4 * H_A].set(a_log.reshape(-1).astype(F32))
    par = par.at[1, 2 * H_A:4 * H_A].set(dt_bias.reshape(-1).astype(F32))
    blk = pl.BlockSpec((ROW_BLOCK, AUX_W), lambda i: (i, 0))
    return pl.pallas_call(
        _gates_kernel,
        out_shape=jax.ShapeDtypeStruct((m, AUX_W), F32),
        grid=(m // ROW_BLOCK,),
        in_specs=[blk, pl.BlockSpec((2, AUX_W), lambda i: (0, 0))],
        out_specs=blk,
        compiler_params=_cparams(("parallel",)),
        name="delta_gates",
    )(ab, par)


_HALO = 16


def _conv_kernel(prev_ref, cur_ref, next_ref, w_ref, o_ref, *, cb, nb, ncb, n_qk_blocks, n_q_blocks):
    t = pl.program_id(1)
    c = pl.program_id(2)
    prev_ok = jnp.logical_and(t != 0, t != ncb)
    next_ok = jnp.logical_and(t != ncb - 1, t != nb - 1)
    prev = jnp.where(prev_ok, prev_ref[0].astype(F32), 0.0)
    nxt = jnp.where(next_ok, next_ref[0].astype(F32), 0.0)
    x = jnp.concatenate([prev, cur_ref[0].astype(F32), nxt], axis=0)
    rows = x.shape[0]
    w = w_ref[...]
    y = None
    for d in range(CONV_W):
        shift = (CONV_W // 2 - d) % rows
        xs = x if shift == 0 else pltpu.roll(x, shift, 0)
        term = xs[_HALO:_HALO + ROW_BLOCK] * w[d:d + 1, :]
        y = term if y is None else y + term
    y = _silu(y)
    is_qk = c < n_qk_blocks
    q_scale = jnp.where(c < n_q_blocks, jnp.float32(DK_A ** -0.5), jnp.float32(1.0))
    for hh in range(cb // DK_A):
        s = y[:, hh * DK_A:(hh + 1) * DK_A]
        inv = lax.rsqrt(jnp.sum(s * s, axis=-1, keepdims=True) + EPS) * q_scale
        o_ref[0, :, hh * DK_A:(hh + 1) * DK_A] = (s * jnp.where(is_qk, inv, 1.0)).astype(o_ref.dtype)


def _conv_prep(p3, conv_w, nb, ncb, cb=512):
    b, lt, _ = p3.shape
    wc = conv_w.shape[1]
    per = ROW_BLOCK // _HALO
    n_halo = lt // _HALO
    kern = functools.partial(_conv_kernel, cb=cb, nb=nb, ncb=ncb,
                             n_qk_blocks=2 * H_A * DK_A // cb, n_q_blocks=H_A * DK_A // cb)
    return pl.pallas_call(
        kern,
        out_shape=jax.ShapeDtypeStruct((b, lt, wc), BF16),
        grid=(b, nb, wc // cb),
        in_specs=[
            pl.BlockSpec((1, _HALO, cb), lambda i, t, c: (i, jnp.maximum(t * per - 1, 0), c)),
            pl.BlockSpec((1, ROW_BLOCK, cb), lambda i, t, c: (i, t, c)),
            pl.BlockSpec((1, _HALO, cb), lambda i, t, c: (i, jnp.minimum((t + 1) * per, n_halo - 1), c)),
            pl.BlockSpec((CONV_W, cb), lambda i, t, c: (0, c)),
        ],
        out_specs=pl.BlockSpec((1, ROW_BLOCK, cb), lambda i, t, c: (i, t, c)),
        compiler_params=_cparams(("parallel", "parallel", "parallel")),
        name="conv_prep",
    )(p3, p3, p3, conv_w)


def _delta_prepare(items, wq_ref, u_ref, at_ref, kd_ref, ge_ref):
    c = CHUNK_A
    row = lax.broadcasted_iota(jnp.int32, (c, 2 * c), 0)
    col = lax.broadcasted_iota(jnp.int32, (c, 2 * c), 1) % c
    eye = (row == col).astype(F32)
    decay, t_inv, pw = [], [], []
    for (_, reverse, q, k, v, beta, gc, gc_row) in items:
        incl = (row <= col) if reverse else (row >= col)
        decay.append(jnp.exp(jnp.where(incl, gc - gc_row, -1e30)))
    gram = [_dot_nt(jnp.concatenate([it[3], it[2]], axis=0), jnp.concatenate([it[3], it[3]], axis=0))
            for it in items]
    kk = [gm[0:c] for gm in gram]
    qk = [gm[c:2 * c] for gm in gram]
    for i, (_, reverse, q, k, v, beta, gc, gc_row) in enumerate(items):
        strict = (row < col) if reverse else (row > col)
        n = jnp.where(strict, beta * kk[i] * decay[i], 0.0)
        pw.append(n)
        t_inv.append(eye - n)
    pw = [_dot3(p, p) for p in pw]
    span = 4
    while span < c:
        prod = [_dot3(jnp.concatenate([p, t], axis=0), p) for p, t in zip(pw, t_inv)]
        pw = [r[0:c] for r in prod]
        t_inv = [t + r[c:2 * c] for t, r in zip(t_inv, prod)]
        span *= 2
    t_inv = [t + _dot3(t, p) for t, p in zip(t_inv, pw)]
    rhs, e_gc = [], []
    for (_, reverse, q, k, v, beta, gc, gc_row) in items:
        e = jnp.exp(gc)
        e_gc.append(e)
        r = jnp.concatenate([v.astype(F32) * beta, k.astype(F32) * (beta * e)], axis=-1).astype(BF16)
        rhs.append(jnp.concatenate([r, r], axis=0))
    uw = [_dot(_hilo_lanes(t), r) for t, r in zip(t_inv, rhs)]
    for i, (slot, reverse, q, k, v, beta, gc, gc_row) in enumerate(items):
        last = 0 if reverse else c - 1
        g_last = gc[last:last + 1, :]
        u_ref[slot] = uw[i][:, :DV_A]
        wq_ref[slot, 0:c, :] = uw[i][:, DV_A:].astype(BF16)
        wq_ref[slot, c:2 * c, :] = (q.astype(F32) * e_gc[i]).astype(BF16)
        at_ref[slot] = (qk[i] * decay[i])[:, 0:c].astype(BF16)
        kd_ref[slot] = (k.astype(F32) * jnp.exp(g_last - gc)).astype(BF16)
        ge_ref[slot] = jnp.broadcast_to(jnp.exp(g_last), ge_ref.shape[1:])


def _delta_kernel(qf_ref, kf_ref, vf_ref, gcf_ref, grf_ref, qb_ref, kb_ref, vb_ref, gcb_ref, grb_ref,
                  of_ref, ob_ref, s_ref, wq_ref, u_ref, at_ref, kd_ref, ge_ref, *, heads):
    @pl.when(pl.program_id(2) == 0)
    def _():
        s_ref[...] = jnp.zeros_like(s_ref)

    c = CHUNK_A
    nchunks = ROW_BLOCK // c
    fwd = (qf_ref, kf_ref, vf_ref, gcf_ref, grf_ref, of_ref)
    bwd = (qb_ref, kb_ref, vb_ref, gcb_ref, grb_ref, ob_ref)

    def slot(hh, reverse, cc):
        return (hh * 2 + int(reverse)) * nchunks + cc

    for h0 in range(0, heads, DELTA_WAVE):
        items = []
        for hh in range(h0, min(h0 + DELTA_WAVE, heads)):
            for reverse in (False, True):
                q_ref, k_ref, v_ref, gc_ref, gr_ref, _ = bwd if reverse else fwd
                bcol, gcol = (1, 3) if reverse else (0, 2)
                hs = slice(hh * DK_A, (hh + 1) * DK_A)
                for cc in range(nchunks):
                    rows = slice(cc * c, (cc + 1) * c)
                    gcols = gc_ref[0, hh, rows, :]
                    items.append((slot(hh, reverse, cc), reverse, q_ref[0, rows, hs], k_ref[0, rows, hs],
                                  v_ref[0, rows, hs], gcols[:, bcol:bcol + 1], gcols[:, gcol:gcol + 1],
                                  gr_ref[0, hh, cc, gcol:gcol + 1, :]))
        _delta_prepare(items, wq_ref, u_ref, at_ref, kd_ref, ge_ref)

    chains = [(hh, reverse) for hh in range(heads) for reverse in (False, True)]
    states = [s_ref[hh * 2 + int(reverse)] for hh, reverse in chains]
    for ci in range(nchunks):
        slots = [slot(hh, reverse, nchunks - 1 - ci if reverse else ci) for hh, reverse in chains]
        sbs = [s.astype(BF16) for s in states]
        ws = [_dot(wq_ref[sl], sb) for sl, sb in zip(slots, sbs)]
        vns = [(u_ref[sl] - r[0:c]).astype(BF16) for sl, r in zip(slots, ws)]
        intra = [_dot(at_ref[sl], vn) for sl, vn in zip(slots, vns)]
        upd = [_dot_tn(kd_ref[sl], vn) for sl, vn in zip(slots, vns)]
        for j, (hh, reverse) in enumerate(chains):
            cc = nchunks - 1 - ci if reverse else ci
            o_ref = bwd[5] if reverse else fwd[5]
            o_ref[0, cc * c:(cc + 1) * c, hh * DV_A:(hh + 1) * DV_A] = (ws[j][c:2 * c] + intra[j]).astype(o_ref.dtype)
            states[j] = states[j] * ge_ref[slots[j], 0:1, :] + upd[j]
    for j, (hh, reverse) in enumerate(chains):
        s_ref[hh * 2 + int(reverse)] = states[j]


def _bwd_block(t, nb, ncb):
    return jnp.where(t < ncb, ncb - 1 - t, nb - 1 - (t - ncb))


def _delta_scan(qkv, g_col, g_row, nb, ncb):
    b, lt, _ = qkv.shape
    nch = ROW_BLOCK // CHUNK_A
    hg = _pick(H_A, (DELTA_HEADS, 2, 1))
    ngrp = H_A // hg
    n_slots = hg * 2 * nch

    def blk(rev, t):
        return _bwd_block(t, nb, ncb) if rev else t

    def tok(grp0, rev):
        return pl.BlockSpec((1, ROW_BLOCK, hg * DK_A), lambda i, h, t: (i, blk(rev, t), grp0 + h))

    def gcol(rev):
        return pl.BlockSpec((1, hg, ROW_BLOCK, 4), lambda i, h, t: (i, h, blk(rev, t), 0))

    def grow(rev):
        return pl.BlockSpec((1, hg, nch, 4, 2 * CHUNK_A), lambda i, h, t: (i, h, blk(rev, t), 0, 0))

    in_specs, args = [], []
    for rev in (False, True):
        in_specs += [tok(0, rev), tok(ngrp, rev), tok(2 * ngrp, rev), gcol(rev), grow(rev)]
        args += [qkv, qkv, qkv, g_col, g_row]
    o_shape = jax.ShapeDtypeStruct((b, lt, H_A * DV_A), BF16)
    return pl.pallas_call(
        functools.partial(_delta_kernel, heads=hg),
        out_shape=[o_shape, o_shape],
        grid=(b, ngrp, nb),
        in_specs=in_specs,
        out_specs=[tok(0, False), tok(0, True)],
        scratch_shapes=[
            pltpu.VMEM((hg * 2, DK_A, DV_A), F32),
            pltpu.VMEM((n_slots, 2 * CHUNK_A, DK_A), BF16),
            pltpu.VMEM((n_slots, CHUNK_A, DV_A), F32),
            pltpu.VMEM((n_slots, CHUNK_A, CHUNK_A), BF16),
            pltpu.VMEM((n_slots, CHUNK_A, DK_A), BF16),
            pltpu.VMEM((n_slots, 8, LANES), F32),
        ],
        compiler_params=_cparams(("parallel", "parallel", "arbitrary")),
        name="delta_scan",
    )(*args)


def _delta_out_kernel(of_ref, ob_ref, z_ref, w_ref, o_ref, *, cb):
    o = of_ref[...].astype(F32) + ob_ref[...].astype(F32)
    z = z_ref[...].astype(F32)
    w = w_ref[...]
    for hh in range(cb // DV_A):
        sl = slice(hh * DV_A, (hh + 1) * DV_A)
        s = o[:, sl]
        y = s * lax.rsqrt(jnp.mean(s * s, axis=-1, keepdims=True) + EPS) * w
        o_ref[:, sl] = (y * _silu(z[:, sl])).astype(o_ref.dtype)


def _delta_out(o_f, o_b, p, col_z, dn_norm_w, cb=512):
    m, wa = o_f.shape
    blk = pl.BlockSpec((ROW_BLOCK, cb), lambda i, c: (i, c))
    j0 = col_z // cb
    return pl.pallas_call(
        functools.partial(_delta_out_kernel, cb=cb),
        out_shape=jax.ShapeDtypeStruct((m, wa), BF16),
        grid=(m // ROW_BLOCK, wa // cb),
        in_specs=[blk, blk, pl.BlockSpec((ROW_BLOCK, cb), lambda i, c: (i, j0 + c)),
                  pl.BlockSpec((1, DV_A), lambda i, c: (0, 0))],
        out_specs=blk,
        compiler_params=_cparams(("parallel", "parallel")),
        name="delta_out",
    )(o_f, o_b, p, dn_norm_w.reshape(1, DV_A).astype(F32))


def _rotate(t, cos, sin):
    half = t.shape[-1] // 2
    t1, t2 = t[:, :half], t[:, half:]
    return jnp.concatenate([t1 * cos - t2 * sin, t2 * cos + t1 * sin], axis=-1)


def _ret_kernel(qf_ref, kf_ref, vf_ref, cf_ref, sf_ref, qb_ref, kb_ref, vb_ref, cb_ref, sb_ref, lg_ref, dec_ref,
                of_ref, ob_ref, stf_ref, stb_ref):
    @pl.when(pl.program_id(2) == 0)
    def _():
        stf_ref[...] = jnp.zeros_like(stf_ref)
        stb_ref[...] = jnp.zeros_like(stb_ref)

    c = ROW_BLOCK
    k_scale = DK_B ** -0.5
    pos0 = lax.broadcasted_iota(jnp.int32, (c, 1), 0)
    dirs = ((False, qf_ref, kf_ref, vf_ref, cf_ref, sf_ref, of_ref, stf_ref),
            (True, qb_ref, kb_ref, vb_ref, cb_ref, sb_ref, ob_ref, stb_ref))
    decays = [dec_ref[0, 0], dec_ref[0, 1]]
    qs, ks, lgs, posfs = [], [], [], []
    for reverse, q_ref, k_ref, v_ref, c_ref, s_ref, o_ref, st_ref in dirs:
        cos, sin = c_ref[...], s_ref[...]
        qs.append(_rotate(q_ref[0].astype(F32), cos, sin))
        ks.append(_rotate(k_ref[0].astype(F32) * k_scale, cos, sin))
        lg = lg_ref[0, 1 if reverse else 0, 0:1, 0:1]
        lgs.append(lg)
        posfs.append(((c - 1 - pos0) if reverse else pos0).astype(F32))
    vs = [d[3][0] for d in dirs]
    states = [d[7][...] for d in dirs]
    qk = [_dot_nt(q.astype(BF16), k.astype(BF16)) for q, k in zip(qs, ks)]
    inter = [_dot((q * jnp.exp((p + 1.0) * lg)).astype(BF16), s.astype(BF16))
             for q, p, lg, s in zip(qs, posfs, lgs, states)]
    upd = [_dot_tn((k * jnp.exp((c - 1.0 - p) * lg)).astype(BF16), v)
           for k, p, lg, v in zip(ks, posfs, lgs, vs)]
    intra = [_dot((a * dec).astype(BF16), v) for a, dec, v in zip(qk, decays, vs)]
    for i, d in enumerate(dirs):
        d[6][0] = (intra[i] + inter[i]).astype(d[6].dtype)
        d[7][...] = states[i] * jnp.exp(c * lgs[i]) + upd[i]


def _ret_scan(p3, col_q, col_k, col_v, cos, sin, lg, dec, nb, ncb):
    b, lt, _ = p3.shape
    jq, jk, jv = col_q // DK_B, col_k // DK_B, col_v // DV_B

    def tok(j0, rev):
        if rev:
            return pl.BlockSpec((1, ROW_BLOCK, DK_B), lambda i, h, t: (i, _bwd_block(t, nb, ncb), j0 + h))
        return pl.BlockSpec((1, ROW_BLOCK, DK_B), lambda i, h, t: (i, t, j0 + h))

    def ang(rev):
        if rev:
            return pl.BlockSpec((ROW_BLOCK, DK_B // 2), lambda i, h, t: (_bwd_block(t, nb, ncb), 0))
        return pl.BlockSpec((ROW_BLOCK, DK_B // 2), lambda i, h, t: (t, 0))

    in_specs, args = [], []
    for rev in (False, True):
        in_specs += [tok(jq, rev), tok(jk, rev), tok(jv, rev), ang(rev), ang(rev)]
        args += [p3, p3, p3, cos, sin]
    in_specs.append(pl.BlockSpec((1, 2, 8, LANES), lambda i, h, t: (h, 0, 0, 0)))
    in_specs.append(pl.BlockSpec((1, 2, ROW_BLOCK, ROW_BLOCK), lambda i, h, t: (h, 0, 0, 0)))
    args += [lg, dec]
    o_shape = jax.ShapeDtypeStruct((b, lt, H_B * DV_B), BF16)
    return pl.pallas_call(
        _ret_kernel,
        out_shape=[o_shape, o_shape],
        grid=(b, H_B, nb),
        in_specs=in_specs,
        out_specs=[tok(0, False), tok(0, True)],
        scratch_shapes=[pltpu.VMEM((DK_B, DV_B), F32), pltpu.VMEM((DK_B, DV_B), F32)],
        compiler_params=_cparams(("parallel", "parallel", "arbitrary")),
        name="retention_scan",
    )(*args)


def _ret_out_kernel(of_ref, ob_ref, g_ref, w_ref, b_ref, o_ref, *, cb):
    o = of_ref[...].astype(F32) + ob_ref[...].astype(F32)
    g = g_ref[...].astype(F32)
    for hh in range(cb // DV_B):
        sl = slice(hh * DV_B, (hh + 1) * DV_B)
        s = o[:, sl]
        d = s - jnp.mean(s, axis=-1, keepdims=True)
        y = d * lax.rsqrt(jnp.mean(d * d, axis=-1, keepdims=True) + GN_EPS)
        o_ref[:, sl] = ((y * w_ref[:, sl] + b_ref[:, sl]) * _silu(g[:, sl])).astype(o_ref.dtype)


def _ret_out(o_f, o_b, p, col_g, gn_w, gn_b, cb=512):
    m, wb = o_f.shape
    blk = pl.BlockSpec((ROW_BLOCK, cb), lambda i, c: (i, c))
    par = pl.BlockSpec((1, cb), lambda i, c: (0, c))
    j0 = col_g // cb
    return pl.pallas_call(
        functools.partial(_ret_out_kernel, cb=cb),
        out_shape=jax.ShapeDtypeStruct((m, wb), BF16),
        grid=(m // ROW_BLOCK, wb // cb),
        in_specs=[blk, blk, pl.BlockSpec((ROW_BLOCK, cb), lambda i, c: (i, j0 + c)), par, par],
        out_specs=blk,
        compiler_params=_cparams(("parallel", "parallel")),
        name="retention_out",
    )(o_f, o_b, p, gn_w.reshape(1, wb).astype(F32), gn_b.reshape(1, wb).astype(F32))


def _rope_tables(ctx_len, seq):
    n_freq = DK_B // 4
    freqs = ROPE_BASE ** (-jnp.arange(n_freq, dtype=F32) / n_freq)
    rows = seq // GRID_W
    r = jnp.repeat(jnp.arange(rows, dtype=F32), GRID_W)
    col = jnp.tile(jnp.arange(GRID_W, dtype=F32), rows)
    ang = jnp.concatenate([r[:, None] * freqs, col[:, None] * freqs], axis=-1)
    ang = jnp.concatenate([jnp.zeros((ctx_len, DK_B // 2), F32), ang], axis=0)
    return jnp.cos(ang), jnp.sin(ang)


def _retention_log_decays():
    h = jnp.arange(H_B, dtype=F32)
    fwd = jnp.log1p(-jnp.exp2(-5.0 - h))
    bwd = jnp.log1p(-jnp.exp2(-5.5 - h))
    lg = jnp.stack([fwd, bwd], axis=1)
    pos = jnp.arange(ROW_BLOCK, dtype=F32)
    dist = pos[:, None] - pos[None, :]
    dist = jnp.stack([dist, -dist])[None]
    dec = jnp.exp(jnp.where(dist >= 0, dist * lg[:, :, None, None], -jnp.inf))
    return jnp.broadcast_to(lg[:, :, None, None], (H_B, 2, 8, LANES)), dec


def kernel(x, c, ctx, c_ctx, mod_down, mod_up, mod_bias, norm_gains, w_in, conv_w, a_log, dt_bias, dn_norm_w,
           gn_w, gn_b, w_br_a, w_br_b, w_out, ffn_gate, ffn_up, ffn_down, router, moe_gate, moe_up, moe_down):
    b, seq, d = x.shape
    ctx_len = ctx.shape[1]
    depth = w_in.shape[0]
    lt = ctx_len + seq
    assert ctx_len % ROW_BLOCK == 0 and seq % ROW_BLOCK == 0 and seq % GRID_W == 0 and b + 1 <= 8
    nb, ncb = lt // ROW_BLOCK, ctx_len // ROW_BLOCK
    m = b * lt
    w_conv = 2 * H_A * DK_A + H_A * DV_A
    w_a, w_b = H_A * DV_A, H_B * DV_B
    n_ab = 4 * H_A
    col_z = w_conv
    n_p1 = col_z + w_a
    col_kb = H_B * DK_B
    col_vb = col_kb + H_B * DK_B
    col_gb = col_vb + w_b
    col_ga = col_gb + w_b
    col_gate_b = col_ga + d
    n_p2 = col_gate_b + d

    tm = _pick(m, (768, 512, 256))

    cond = jnp.zeros((8, d), F32).at[:b].set(c).at[b].set(c_ctx)
    mod = _modulation(cond, mod_down, mod_up, mod_bias)[:, :b + 1].reshape(depth, b + 1, 6, d)

    cos, sin = _rope_tables(ctx_len, seq)
    lg, dec = _retention_log_decays()

    w_br_a, w_br_b, w_out = _to_bf16(w_br_a), _to_bf16(w_br_b), _to_bf16(w_out)
    ffn_gate, ffn_up, ffn_down = _to_bf16(ffn_gate), _to_bf16(ffn_up), _to_bf16(ffn_down)
    moe_gate, moe_up, moe_down = _to_bf16(moe_gate), _to_bf16(moe_up), _to_bf16(moe_down)

    def in_weights(i):
        w = w_in[i]
        w_ab = jnp.zeros((d, AUX_W), BF16).at[:, :n_ab].set(w[:, n_p1:n_p1 + n_ab].astype(BF16))
        return w[:, :n_p1].astype(BF16), w[:, n_p1 + n_ab:].astype(BF16), w_ab

    def mix_params(i, idx):
        return jnp.stack([mod[i, :, idx[0]], mod[i, :, idx[1]], mod[i, :, idx[2]]], axis=1)

    w_p1, w_p2, w_ab = in_weights(0)
    mp0 = jnp.stack([jnp.zeros_like(mod[0, :, 0]), mod[0, :, 0], mod[0, :, 1]], axis=1)
    xs, h, ab = _row_stage(x.reshape(b * seq, d), None, mp0, jnp.stack([norm_gains[0, 0], norm_gains[0, 0]]),
                           w_ab, nb, ncb, has_h=True, ctx_rows=ctx.reshape(b * ctx_len, d))

    for i in range(depth):
        gains = norm_gains[i]
        tn_in = (1024, 512, 256, 128)
        p = _matmul(h, w_p1[None], 0, tm=tm, tn=_pick(n_p1, tn_in), name="in_proj_a")
        p2 = _matmul(h, w_p2[None], 0, tm=tm, tn=_pick(n_p2, tn_in), name="in_proj_b")
        p3 = p.reshape(b, lt, n_p1)
        p23 = p2.reshape(b, lt, n_p2)
        g = _gates(ab, a_log[i], dt_bias[i])[:, :n_ab].reshape(b, lt, 4, H_A)
        g_col = g.transpose(0, 3, 1, 2)
        g_row = g.reshape(b, lt // CHUNK_A, CHUNK_A, 4, H_A).transpose(0, 4, 1, 3, 2)
        g_row = jnp.concatenate([g_row, g_row], axis=-1)
        qkv = _conv_prep(p3, conv_w[i], nb, ncb, cb=_pick(H_A * DK_A, (512, 256, 128)))
        of, ob = _delta_scan(qkv, g_col, g_row, nb, ncb)
        o_a = _delta_out(of.reshape(m, w_a), ob.reshape(m, w_a), p, col_z, dn_norm_w[i],
                         cb=_pick(w_a, (512, 256, 128)))
        rf, rb = _ret_scan(p23, 0, col_kb, col_vb, cos, sin, lg, dec, nb, ncb)
        o_b = _ret_out(rf.reshape(m, w_b), rb.reshape(m, w_b), p2, col_gb, gn_w[i], gn_b[i],
                       cb=_pick(w_b, (512, 256)))
        tn_d = _pick(d, (1024, 512, 256, 128))
        merged = _branch_merge(o_a, w_br_a, o_b, w_br_b, i, p2, col_ga, col_gate_b,
                               tm=tm, tn=_pick(d, (512, 256, 128)))
        y = _matmul(merged, w_out, i, tm=tm, tn=tn_d, name="out_proj")
        j = i // 2
        is_moe = i % 2 == 1
        gn = jnp.stack([gains[1], gains[2]])
        if is_moe:
            n_exp = router.shape[2]
            w_r = jnp.zeros((d, AUX_W), BF16).at[:, :n_exp].set(router[j].astype(BF16))
            xs, h2, logits = _row_stage(xs, y, mix_params(i, (2, 3, 4)), gn, w_r, nb, ncb, has_h=True)
            combine = _router(logits, n_exp)
            f_e = moe_gate.shape[3]
            wg = moe_gate.reshape(-1, d, f_e)
            wu = moe_up.reshape(-1, d, f_e)
            wd = moe_down.reshape(-1, n_exp * f_e, d)
        else:
            xs, h2 = _row_stage(xs, y, mix_params(i, (2, 3, 4)), gn, None, nb, ncb, has_h=True)
            combine = None
            n_exp = 1
            wg, wu, wd = ffn_gate, ffn_up, ffn_down
        act = _ffn_up(h2, wg, wu, j, n_exp, combine, tm=tm, tn=_pick(wg.shape[2], (512, 256, 128)))
        y2 = _matmul(act, wd, j, tm=tm, tn=_pick(d, (512, 256, 128)), name="ffn_down")
        if i + 1 < depth:
            w_p1, w_p2, w_ab = in_weights(i + 1)
            mp = jnp.stack([mod[i, :, 5], mod[i + 1, :, 0], mod[i + 1, :, 1]], axis=1)
            gn = jnp.stack([gains[3], norm_gains[i + 1, 0]])
            xs, h, ab = _row_stage(xs, y2, mp, gn, w_ab, nb, ncb, has_h=True)
        else:
            mp = jnp.stack([mod[i, :, 5], mod[i, :, 5], mod[i, :, 5]], axis=1)
            (out,) = _row_stage(xs, y2, mp, jnp.stack([gains[3], gains[3]]), None, nb, ncb, has_h=False,
                                latent_only=True)

    return out.reshape(b, seq, d)
```
